```python
import math
import jax
import jax.numpy as jnp
from jax import lax
import numpy as np

D_MODEL = 2048
BATCH = 1
SEQ = 8192
DEPTH = 2

GRID_W = 64
CTX_LEN = 256
N_MOD = 9
D_FF = 5632
NORM_EPS = 1e-6

GLA_WIDTH = D_MODEL // 2
GLA_HEADS = 4
GLA_DK = GLA_WIDTH // (2 * GLA_HEADS)
GLA_DV = GLA_WIDTH // GLA_HEADS
GLA_QK = GLA_HEADS * GLA_DK
GLA_GATE_RANK = 16
GLA_TAU = 16.0
GLA_CHUNK = 64

POOL_WIDTH = D_MODEL - GLA_WIDTH
POOL_WINDOWS = (2, 4, 8, 16)
POOL_GROUPS = 4
POOL_GC = POOL_WIDTH // POOL_GROUPS

IN0_COLS = 2 * GLA_QK + 2 * GLA_WIDTH + 2 * GLA_GATE_RANK + POOL_WIDTH
SPLIT0 = (GLA_QK,
          2 * GLA_QK,
          2 * GLA_QK + GLA_WIDTH,
          2 * GLA_QK + 2 * GLA_WIDTH,
          2 * GLA_QK + 2 * GLA_WIDTH + GLA_GATE_RANK,
          2 * GLA_QK + 2 * GLA_WIDTH + 2 * GLA_GATE_RANK)

DIFF_HEADS = 8
DIFF_DH = D_MODEL // (2 * DIFF_HEADS)
DIFF_DV = 2 * DIFF_DH
Q_BLOCK = 128
ROPE_THETA = 10000.0

N_EVEN = (DEPTH + 1) // 2
N_ODD = DEPTH // 2

kernel_name = "hybrid_gla_pool_diffattn_dit"


def _rms_norm(x):
    x32 = x.astype(jnp.float32)
    y = x32 * lax.rsqrt(jnp.mean(x32 * x32, axis=-1, keepdims=True) + NORM_EPS)
    return y.astype(x.dtype)


def _modulate(x, shift, scale):
    return x * (1 + scale) + shift


def _swiglu(z, w1, w3, w2):
    return (jax.nn.silu(z @ w1) * (z @ w3)) @ w2


def _ffn_half(h, m, slot, w1, w3, w2):
    z = _modulate(_rms_norm(h), m[:, :, 3 * slot], m[:, :, 3 * slot + 1])
    return h + 0.5 * m[:, :, 3 * slot + 2] * _swiglu(z, w1, w3, w2)


def _gla_scan(q, k, v, g, s0):
    f32 = jnp.float32
    B, L, H, _ = q.shape
    n = L // GLA_CHUNK

    def to_chunks(a):
        return jnp.moveaxis(a.astype(f32).reshape(B, n, GLA_CHUNK, H, a.shape[-1]), 1, 0)

    qc, kc, vc, gc = to_chunks(q), to_chunks(k), to_chunks(v), to_chunks(g)
    causal = jnp.tril(jnp.ones((GLA_CHUNK, GLA_CHUNK), bool))[None, :, :, None, None]

    def step(S, inp):
        qb, kb, vb, gb = inp
        b = jnp.cumsum(gb, axis=1)
        inter = jnp.einsum('bthk,bhkv->bthv', qb * jnp.exp(b), S)
        diff = b[:, :, None] - b[:, None, :]
        decay = jnp.exp(jnp.where(causal, diff, -jnp.inf))
        attn = jnp.einsum('bthk,bshk,btshk->bhts', qb, kb, decay)
        intra = jnp.einsum('bhts,bshv->bthv', attn, vb)
        b_last = b[:, -1]
        S_new = jnp.exp(b_last)[..., None] * S + jnp.einsum(
            'bshk,bshv->bhkv', kb * jnp.exp(b_last[:, None] - b), vb)
        return S_new, inter + intra

    S_fin, out = lax.scan(step, s0, (qc, kc, vc, gc))
    out = jnp.moveaxis(out, 0, 1).reshape(B, L, H, v.shape[-1])
    return out, S_fin


def _gla_direction(ctx_in, lat_in, reverse):
    if reverse:
        ctx_in = tuple(jnp.flip(a, axis=1) for a in ctx_in)
        lat_in = tuple(jnp.flip(a, axis=1) for a in lat_in)
    B = lat_in[0].shape[0]
    s0 = jnp.zeros((B, GLA_HEADS, GLA_DK, GLA_DV), jnp.float32)
    o_ctx, s_ctx = _gla_scan(*ctx_in, s0)
    o_lat, _ = _gla_scan(*lat_in, s_ctx)
    if reverse:
        o_ctx = jnp.flip(o_ctx, axis=1)
        o_lat = jnp.flip(o_lat, axis=1)
    return o_ctx, o_lat


def _multiscale_pool(u):
    B, L, _ = u.shape
    u32 = u.astype(jnp.float32)
    cs = jnp.concatenate([jnp.zeros((B, 1, POOL_WIDTH), jnp.float32), jnp.cumsum(u32, axis=1)], axis=1)
    t = jnp.arange(L)
    outs = []
    for gi, w in enumerate(POOL_WINDOWS):
        lo = jnp.clip(t - w // 2, 0, L)
        hi = jnp.clip(t + (w - w // 2), 0, L)
        sl = slice(gi * POOL_GC, (gi + 1) * POOL_GC)
        s = cs[:, hi, sl] - cs[:, lo, sl]
        cnt = (hi - lo).astype(jnp.float32)[None, :, None]
        outs.append(s / cnt - u32[..., sl])
    return jnp.stack(outs, axis=2)


def _mixer_even(a, ac, w_in, gate_w2, gate_b, norm_w, pool_w, pool_scale, w_out, need_ctx):
    def project(z):
        B, L, _ = z.shape
        q, k, v, r, gf, gb, u = jnp.split(z @ w_in, SPLIT0, axis=-1)
        q = q.reshape(B, L, GLA_HEADS, GLA_DK) * (GLA_DK ** -0.5)
        k = k.reshape(B, L, GLA_HEADS, GLA_DK)
        v = v.reshape(B, L, GLA_HEADS, GLA_DV)

        def log_gate(gz, d):
            zz = (gz @ gate_w2[d] + gate_b[d]).astype(jnp.float32)
            return (jax.nn.log_sigmoid(zz) / GLA_TAU).reshape(B, L, GLA_HEADS, GLA_DK)

        return q, k, v, r, log_gate(gf, 0), log_gate(gb, 1), u

    q, k, v, r, lf, lb, u = project(a)
    qc, kc, vc, rc, lfc, lbc, uc = project(ac)
    ocf, olf = _gla_direction((qc, kc, vc, lfc), (q, k, v, lf), reverse=False)
    ocb, olb = _gla_direction((qc, kc, vc, lbc), (q, k, v, lb), reverse=True)

    def readout(o, r, u):
        B, L = o.shape[:2]
        o = _rms_norm(o) * norm_w
        g = o.astype(r.dtype).reshape(B, L, GLA_WIDTH) * jax.nn.silu(r)
        pooled = _multiscale_pool(u).astype(u.dtype)
        p = jnp.einsum('blgc,gcd->blgd', pooled, pool_w).reshape(B, L, POOL_WIDTH) * pool_scale
        return jnp.concatenate([g, p], axis=-1) @ w_out

    y = readout(olf + olb, r, u)
    yc = readout(ocf + ocb, rc, uc) if need_ctx else None
    return y, yc


def _axial_rope_tables(L):
    rows = L // GRID_W
    row = jnp.repeat(jnp.arange(rows), GRID_W).astype(jnp.float32)
    col = jnp.tile(jnp.arange(GRID_W), rows).astype(jnp.float32)
    n_freq = DIFF_DH // 4
    inv_freq = ROPE_THETA ** (-jnp.arange(n_freq, dtype=jnp.float32) / n_freq)
    ang = jnp.concatenate([row[:, None] * inv_freq, col[:, None] * inv_freq], axis=-1)
    return jnp.cos(ang), jnp.sin(ang)


def _rope(x, cos, sin):
    half = x.shape[-1] // 2
    x1, x2 = x[..., :half], x[..., half:]
    c = cos[None, :, None, None, :].astype(x.dtype)
    s = sin[None, :, None, None, :].astype(x.dtype)
    return jnp.concatenate([x1 * c - x2 * s, x1 * s + x2 * c], axis=-1)


def _mixer_odd(a, ac, w_qkv, lam_p, norm_w, w_out, lam_init, need_ctx):
    B, L, _ = a.shape
    scale = DIFF_DH ** -0.5
    lp = lam_p.astype(jnp.float32)
    lam = jnp.exp(jnp.sum(lp[0] * lp[1])) - jnp.exp(jnp.sum(lp[2] * lp[3])) + lam_init

    def project(z):
        Bz, Lz, _ = z.shape
        q, k, v = jnp.split(z @ w_qkv, 3, axis=-1)
        return (q.reshape(Bz, Lz, DIFF_HEADS, 2, DIFF_DH),
                k.reshape(Bz, Lz, DIFF_HEADS, 2, DIFF_DH),
                v.reshape(Bz, Lz, DIFF_HEADS, DIFF_DV))

    q, k, v = project(a)
    qc, kc, vc = project(ac)
    cos, sin = _axial_rope_tables(L)
    q = _rope(q, cos, sin)
    k = _rope(k, cos, sin)
    k_all = jnp.concatenate([kc, k], axis=1)
    v_all = jnp.concatenate([vc, v], axis=1)

    def attend(qb, keys, vals):
        s = jnp.einsum('bqhcd,bkhcd->bchqk', qb, keys).astype(jnp.float32) * scale
        p = jax.nn.softmax(s, axis=-1)
        w = p[:, 0] - lam * p[:, 1]
        return jnp.einsum('bhqk,bkhv->bqhv', w.astype(vals.dtype), vals)

    nb = L // Q_BLOCK
    qblocks = jnp.moveaxis(q.reshape(B, nb, Q_BLOCK, DIFF_HEADS, 2, DIFF_DH), 1, 0)
    o = lax.map(lambda qb: attend(qb, k_all, v_all), qblocks)
    o = jnp.moveaxis(o, 0, 1).reshape(B, L, DIFF_HEADS, DIFF_DV)

    def readout(o):
        o = _rms_norm(o) * norm_w * (1.0 - lam_init)
        return o.reshape(o.shape[0], o.shape[1], D_MODEL) @ w_out

    y = readout(o)
    yc = readout(attend(qc, kc, vc)) if need_ctx else None
    return y, yc


def setup_inputs(seed: int = 0) -> dict:
    key = jax.random.key(seed)
    ks = jax.random.split(key, 24)
    f32 = jnp.float32
    D = D_MODEL

    def nrm(k, shape, std):
        return jax.random.normal(k, shape, f32) * std

    return {
        "x": nrm(ks[0], (BATCH, SEQ, D), 1.0),
        "c": nrm(ks[1], (BATCH, D), 1.0),
        "ctx": nrm(ks[2], (BATCH, CTX_LEN, D), 1.0),
        "c_ctx": nrm(ks[3], (D,), 1.0),
        "ada_w": nrm(ks[4], (DEPTH, D, N_MOD * D), 0.5 * D ** -0.5),
        "ada_b": nrm(ks[5], (DEPTH, N_MOD * D), 0.02),
        "ffn_w1": nrm(ks[6], (DEPTH, 2, D, D_FF), D ** -0.5),
        "ffn_w3": nrm(ks[7], (DEPTH, 2, D, D_FF), D ** -0.5),
        "ffn_w2": nrm(ks[8], (DEPTH, 2, D_FF, D), D_FF ** -0.5),
        "gla_w_in": nrm(ks[9], (N_EVEN, D, IN0_COLS), D ** -0.5),
        "gla_gate_w2": nrm(ks[10], (N_EVEN, 2, GLA_GATE_RANK, GLA_QK), GLA_GATE_RANK ** -0.5),
        "gla_gate_b": nrm(ks[11], (N_EVEN, 2, GLA_QK), 0.1),
        "gla_norm_w": 1.0 + nrm(ks[12], (N_EVEN, GLA_DV), 0.1),
        "pool_w": nrm(ks[13], (N_EVEN, POOL_GROUPS, POOL_GC, POOL_GC), POOL_GC ** -0.5),
        "pool_scale": 1.0 + nrm(ks[14], (N_EVEN, POOL_WIDTH), 0.1),
        "mix0_w_out": nrm(ks[15], (N_EVEN, D, D), D ** -0.5),
        "diff_w_qkv": nrm(ks[16], (N_ODD, D, 3 * D), D ** -0.5),
        "diff_lambda": nrm(ks[17], (N_ODD, 4, DIFF_DH), 0.1),
        "diff_norm_w": 1.0 + nrm(ks[18], (N_ODD, DIFF_DV), 0.1),
        "diff_w_out": nrm(ks[19], (N_ODD, D, D), D ** -0.5),
        "final_norm_w": 1.0 + nrm(ks[20], (D,), 0.1),
    }


def reference(x, c, ctx, c_ctx, ada_w, ada_b, ffn_w1, ffn_w3, ffn_w2,
              gla_w_in, gla_gate_w2, gla_gate_b, gla_norm_w, pool_w, pool_scale, mix0_w_out,
              diff_w_qkv, diff_lambda, diff_norm_w, diff_w_out, final_norm_w):
    h, hc = x, ctx
    B = x.shape[0]
    for i in range(DEPTH):
        last = i == DEPTH - 1
        m = (jax.nn.silu(c) @ ada_w[i] + ada_b[i]).reshape(B, 1, N_MOD, D_MODEL)
        mc = (jax.nn.silu(c_ctx) @ ada_w[i] + ada_b[i]).reshape(1, 1, N_MOD, D_MODEL)
        h = _ffn_half(h, m, 0, ffn_w1[i, 0], ffn_w3[i, 0], ffn_w2[i, 0])
        hc = _ffn_half(hc, mc, 0, ffn_w1[i, 0], ffn_w3[i, 0], ffn_w2[i, 0])
        a = _modulate(_rms_norm(h), m[:, :, 3], m[:, :, 4])
        ac = _modulate(_rms_norm(hc), mc[:, :, 3], mc[:, :, 4])
        j = i // 2
        if i % 2 == 0:
            y, yc = _mixer_even(a, ac, gla_w_in[j], gla_gate_w2[j], gla_gate_b[j], gla_norm_w[j],
                                pool_w[j], pool_scale[j], mix0_w_out[j], not last)
        else:
            lam_init = 0.8 - 0.6 * math.exp(-0.3 * i)
            y, yc = _mixer_odd(a, ac, diff_w_qkv[j], diff_lambda[j], diff_norm_w[j], diff_w_out[j],
                               lam_init, not last)
        h = h + m[:, :, 5] * y
        h = _ffn_half(h, m, 2, ffn_w1[i, 1], ffn_w3[i, 1], ffn_w2[i, 1])
        if not last:
            hc = hc + mc[:, :, 5] * yc
            hc = _ffn_half(hc, mc, 2, ffn_w1[i, 1], ffn_w3[i, 1], ffn_w2[i, 1])
    return _rms_norm(h) * final_norm_w
```

```python
import functools
import math

import jax
import jax.numpy as jnp
from jax import lax
from jax.experimental import pallas as pl
from jax.experimental.pallas import tpu as pltpu

F32 = jnp.float32
BF16 = jnp.bfloat16

NORM_EPS = 1e-6
GRID_W = 64
ROPE_THETA = 10000.0

GLA_HEADS = 4
GLA_DK = 128
GLA_DV = 256
GLA_GATE_RANK = 16
GLA_TAU = 16.0
GLA_CHUNK = 64
GLA_SUB = 16
GLA_GROUP = 256
POOL_WINDOWS = (2, 4, 8, 16)
POOL_GC = 256
POOL_HALO = 8

DIFF_HEADS = 8
DIFF_DH = 128
DIFF_DV = 256

LANES = 128
MIB = 1024 * 1024


def _cparams(semantics, vmem_mib):
    return pltpu.CompilerParams(dimension_semantics=semantics,
                                vmem_limit_bytes=int(vmem_mib * MIB))


def _rms(x):
    return x * lax.rsqrt(jnp.mean(x * x, axis=-1, keepdims=True) + NORM_EPS)


def _is_ctx_rows(row0, rows, n_lat):
    return (row0 + lax.broadcasted_iota(jnp.int32, (rows, 1), 0)) >= n_lat


def _mod_row(mod_ref, is_ctx, k):
    return jnp.where(is_ctx, mod_ref[1, k:k + 1, :], mod_ref[0, k:k + 1, :])


def _modulated(h, mod_ref, is_ctx):
    return _rms(h) * (1.0 + _mod_row(mod_ref, is_ctx, 1)) + _mod_row(mod_ref, is_ctx, 0)


def _ada_kernel(c_ref, w_ref, b_ref, o_ref):
    c = c_ref[...]
    s = (c * jax.nn.sigmoid(c)).astype(BF16)
    o_ref[0] = jnp.dot(s, w_ref[0].astype(BF16), preferred_element_type=F32) + b_ref[0]


def _ada(c8, ada_w, ada_b):
    depth, d, n = ada_w.shape
    tn = 1024
    return pl.pallas_call(
        _ada_kernel,
        grid=(depth, n // tn),
        in_specs=[pl.BlockSpec((8, d), lambda i, j: (0, 0)),
                  pl.BlockSpec((1, d, tn), lambda i, j: (i, 0, j)),
                  pl.BlockSpec((1, 1, tn), lambda i, j: (i, 0, j))],
        out_specs=pl.BlockSpec((1, 8, tn), lambda i, j: (i, 0, j)),
        out_shape=jax.ShapeDtypeStruct((depth, 8, n), F32),
        compiler_params=_cparams(("arbitrary", "arbitrary"), 40),
        name="ada_mod",
    )(c8, ada_w, ada_b.reshape(depth, 1, n))


def _ffn_kernel(*refs, n_lat, tm, final):
    if final:
        h_ref, mod_ref, w1_ref, w3_ref, w2_ref, fnw_ref, o_ref, z_ref = refs
    else:
        h_ref, mod_ref, w1_ref, w3_ref, w2_ref, o_ref, z_ref = refs
    i = pl.program_id(0)
    j = pl.program_id(1)
    is_ctx = _is_ctx_rows(i * tm, tm, n_lat)

    @pl.when(j == 0)
    def _():
        z_ref[...] = _modulated(h_ref[...], mod_ref, is_ctx).astype(BF16)
        o_ref[...] = jnp.zeros_like(o_ref)

    z = z_ref[...]
    u = jnp.dot(z, w1_ref[...], preferred_element_type=F32)
    g = jnp.dot(z, w3_ref[...], preferred_element_type=F32)
    a = (u * jax.nn.sigmoid(u) * g).astype(BF16)
    o_ref[...] += jnp.dot(a, w2_ref[...], preferred_element_type=F32)

    @pl.when(j == pl.num_programs(1) - 1)
    def _():
        hn = h_ref[...] + 0.5 * _mod_row(mod_ref, is_ctx, 2) * o_ref[...]
        if final:
            hn = _rms(hn) * fnw_ref[...]
        o_ref[...] = hn


def _ffn(h, mod3, w1, w3, w2, *, rows, tm, n_lat, final_w=None):
    d = h.shape[1]
    dff = w1.shape[1]
    tf = 512
    final = final_w is not None
    in_specs = [pl.BlockSpec((tm, d), lambda i, j: (i, 0)),
                pl.BlockSpec((2, 3, d), lambda i, j: (0, 0, 0)),
                pl.BlockSpec((d, tf), lambda i, j: (0, j)),
                pl.BlockSpec((d, tf), lambda i, j: (0, j)),
                pl.BlockSpec((tf, d), lambda i, j: (j, 0))]
    args = [h, mod3, w1, w3, w2]
    if final:
        in_specs.append(pl.BlockSpec((1, d), lambda i, j: (0, 0)))
        args.append(final_w.reshape(1, d))
    return pl.pallas_call(
        functools.partial(_ffn_kernel, n_lat=n_lat, tm=tm, final=final),
        grid=(rows // tm, dff // tf),
        in_specs=in_specs,
        out_specs=pl.BlockSpec((tm, d), lambda i, j: (i, 0)),
        out_shape=jax.ShapeDtypeStruct((rows, d), F32),
        scratch_shapes=[pltpu.VMEM((tm, d), BF16)],
        compiler_params=_cparams(("arbitrary", "arbitrary"), 56),
        name="ffn_final" if final else "ffn_half",
    )(*args)


def _log_sigmoid(x):
    return jnp.minimum(x, 0.0) - jnp.log1p(jnp.exp(-jnp.abs(x)))


def _proj0_kernel(h_ref, mod_ref, w_ref, wg_ref, gw2_ref, gb_ref,
                  p_ref, lf_ref, lb_ref, z_ref, *, n_lat, tm):
    i = pl.program_id(0)
    j = pl.program_id(1)

    @pl.when(j == 0)
    def _():
        is_ctx = _is_ctx_rows(i * tm, tm, n_lat)
        z = _modulated(h_ref[...], mod_ref, is_ctx).astype(BF16)
        z_ref[...] = z
        gz = jnp.dot(z, wg_ref[...], preferred_element_type=F32).astype(BF16)
        for d, out in ((0, lf_ref), (1, lb_ref)):
            zz = jnp.dot(gz, gw2_ref[d], preferred_element_type=F32) + gb_ref[d]
            out[...] = _log_sigmoid(zz) / GLA_TAU

    p_ref[...] = jnp.dot(z_ref[...], w_ref[...], preferred_element_type=F32)


def _proj0(h, mod3, w_main, w_gate, gate_w2p, gate_b, *, tm, n_lat):
    rows, d = h.shape
    n = w_main.shape[1]
    qk = gate_b.shape[-1]
    tn = 512
    return pl.pallas_call(
        functools.partial(_proj0_kernel, n_lat=n_lat, tm=tm),
        grid=(rows // tm, n // tn),
        in_specs=[pl.BlockSpec((tm, d), lambda i, j: (i, 0)),
                  pl.BlockSpec((2, 3, d), lambda i, j: (0, 0, 0)),
                  pl.BlockSpec((d, tn), lambda i, j: (0, j)),
                  pl.BlockSpec((d, LANES), lambda i, j: (0, 0)),
                  pl.BlockSpec((2, LANES, qk), lambda i, j: (0, 0, 0)),
                  pl.BlockSpec((2, 1, qk), lambda i, j: (0, 0, 0))],
        out_specs=[pl.BlockSpec((tm, tn), lambda i, j: (i, j)),
                   pl.BlockSpec((tm, qk), lambda i, j: (i, 0)),
                   pl.BlockSpec((tm, qk), lambda i, j: (i, 0))],
        out_shape=[jax.ShapeDtypeStruct((rows, n), F32),
                   jax.ShapeDtypeStruct((rows, qk), F32),
                   jax.ShapeDtypeStruct((rows, qk), F32)],
        scratch_shapes=[pltpu.VMEM((tm, d), BF16)],
        compiler_params=_cparams(("arbitrary", "arbitrary"), 48),
        name="proj_gla_pool",
    )(h, mod3, w_main, w_gate, gate_w2p, gate_b.reshape(2, 1, qk))


def _split3(x):
    x1 = x.astype(BF16)
    r1 = x - x1.astype(F32)
    x2 = r1.astype(BF16)
    x3 = (r1 - x2.astype(F32)).astype(BF16)
    return x1, x2, x3


def _gla_chunk(q_ref, k_ref, v_ref, g_ref, o_ref, s_ref, b_ref, base, reverse):
    C, SB = GLA_CHUNK, GLA_SUB
    nb = C // SB
    rows = pl.ds(base, C)
    q = q_ref[rows, :] * (GLA_DK ** -0.5)
    k = k_ref[rows, :]
    v = v_ref[rows, :]
    g = g_ref[rows, :]

    ti = lax.broadcasted_iota(jnp.int32, (C, C), 0)
    si = lax.broadcasted_iota(jnp.int32, (C, C), 1)
    tri = jnp.where((si >= ti) if reverse else (si <= ti), 1.0, 0.0).astype(BF16)
    b = sum(jnp.dot(tri, part, preferred_element_type=F32) for part in _split3(g))
    b_ref[...] = b
    s_old = s_ref[...]

    inter = jnp.dot((q * jnp.exp(b)).astype(BF16), s_old.astype(BF16),
                    preferred_element_type=F32)

    col = lax.broadcasted_iota(jnp.int32, (SB, C), 1)
    lane = lax.broadcasted_iota(jnp.int32, (SB, LANES), 1)
    trow = lax.broadcasted_iota(jnp.int32, (SB, LANES), 0)
    ones = jnp.ones((LANES, LANES), BF16)
    a_rows = []
    for i in range(nb):
        blk = slice(i * SB, (i + 1) * SB)
        q_i = q[blk]
        b_i = b[blk]
        edge = (i + 1) * SB if reverse else i * SB - 1
        has_off = (i < nb - 1) if reverse else (i > 0)
        if has_off:
            r_i = b_ref[edge:edge + 1, :]
            qe = (q_i * jnp.exp(b_i - r_i)).astype(BF16)
            ke = (k * jnp.exp(jnp.minimum(r_i - b, 0.0))).astype(BF16)
            raw = lax.dot_general(qe, ke, (((1,), (1,)), ((), ())),
                                  preferred_element_type=F32)
            off_mask = (col >= (i + 1) * SB) if reverse else (col < i * SB)
            a_i = jnp.where(off_mask, raw, 0.0)
        else:
            a_i = jnp.zeros((SB, C), F32)
        terms = []
        for s in range(SB):
            row = base + i * SB + s
            k_s = k_ref[pl.ds(row, 1), :]
            b_s = b_ref[i * SB + s:i * SB + s + 1, :]
            keep = (trow <= s) if reverse else (trow >= s)
            terms.append(jnp.where(keep, q_i * k_s * jnp.exp(b_i - b_s), 0.0))
        stacked = jnp.concatenate(terms, axis=0).astype(BF16)
        dsum = jnp.dot(stacked, ones, preferred_element_type=F32)
        diag = jnp.zeros((SB, LANES), F32)
        for s in range(SB):
            diag = diag + jnp.where(lane == i * SB + s, dsum[s * SB:(s + 1) * SB], 0.0)
        a_rows.append(a_i + diag[:, :C])
    attn = jnp.concatenate(a_rows, axis=0).astype(BF16)
    intra = jnp.dot(attn, v, preferred_element_type=F32)
    o_ref[rows, :] = inter + intra

    last = 0 if reverse else C - 1
    b_last = b_ref[last:last + 1, :]
    ke = (k * jnp.exp(b_last - b)).astype(BF16)
    upd = lax.dot_general(ke, v, (((0,), (0,)), ((), ())), preferred_element_type=F32)
    decay = jnp.broadcast_to(jnp.exp(b_last), (GLA_DK, GLA_DK)).T
    s_ref[...] = jnp.concatenate([decay, decay], axis=1) * s_old + upd


def _gla_kernel(qf_ref, kf_ref, vf_ref, gf_ref, qb_ref, kb_ref, vb_ref, gb_ref,
                of_ref, ob_ref, sf_ref, sb_ref, bf_ref, bb_ref):
    @pl.when(pl.program_id(1) == 0)
    def _():
        sf_ref[...] = jnp.zeros_like(sf_ref)
        sb_ref[...] = jnp.zeros_like(sb_ref)

    n_sub = GLA_GROUP // GLA_CHUNK

    def body(c, carry):
        fbase = pl.multiple_of(c * GLA_CHUNK, GLA_CHUNK)
        rbase = pl.multiple_of((n_sub - 1 - c) * GLA_CHUNK, GLA_CHUNK)
        _gla_chunk(qf_ref, kf_ref, vf_ref, gf_ref, of_ref, sf_ref, bf_ref, fbase, False)
        _gla_chunk(qb_ref, kb_ref, vb_ref, gb_ref, ob_ref, sb_ref, bb_ref, rbase, True)
        return carry

    lax.fori_loop(0, n_sub, body, 0)


def _gla(p0, v16, lf, lb, *, n_lat):
    rows = p0.shape[0]
    n_grp = rows // GLA_GROUP
    lat_grp = n_lat // GLA_GROUP
    G = GLA_GROUP

    def fwd(c):
        return (c + lat_grp) % n_grp

    def bwd(c):
        return n_grp - 1 - c

    def spec(width, order, col):
        return pl.BlockSpec((G, width), lambda h, c: (order(c), col(h)))

    in_specs = []
    for order in (fwd, bwd):
        in_specs += [spec(GLA_DK, order, lambda h: h),
                     spec(GLA_DK, order, lambda h: GLA_HEADS + h),
                     spec(GLA_DV, order, lambda h: h),
                     spec(GLA_DK, order, lambda h: h)]
    out_specs = [spec(GLA_DV, fwd, lambda h: h), spec(GLA_DV, bwd, lambda h: h)]
    width = GLA_HEADS * GLA_DV
    return pl.pallas_call(
        _gla_kernel,
        grid=(GLA_HEADS, n_grp),
        in_specs=in_specs,
        out_specs=out_specs,
        out_shape=[jax.ShapeDtypeStruct((rows, width), F32)] * 2,
        scratch_shapes=[pltpu.VMEM((GLA_DK, GLA_DV), F32), pltpu.VMEM((GLA_DK, GLA_DV), F32),
                        pltpu.VMEM((GLA_CHUNK, GLA_DK), F32), pltpu.VMEM((GLA_CHUNK, GLA_DK), F32)],
        compiler_params=_cparams(("arbitrary", "arbitrary"), 32),
        name="gla_scan",
    )(p0, p0, v16, lf, p0, p0, v16, lb)


def _mix0_out_kernel(h_ref, mod_ref, of_ref, ob_ref, r_ref, uprev_ref, u_ref, unext_ref,
                     nw_ref, pw_ref, ps_ref, wout_ref, o_ref, ext_ref,
                     *, n_lat, n_ctx, tm):
    i = pl.program_id(0)
    lat_tiles = n_lat // tm
    n_tiles = (n_lat + n_ctx) // tm
    in_ctx = i >= lat_tiles
    is_first = (i == 0) | (i == lat_tiles)
    is_last = (i == lat_tiles - 1) | (i == n_tiles - 1)

    o = of_ref[...] + ob_ref[...]
    r = r_ref[...]
    heads = []
    for hd in range(GLA_HEADS):
        heads.append(_rms(o[:, hd * GLA_DV:(hd + 1) * GLA_DV]) * nw_ref[...])
    gl = jnp.concatenate(heads, axis=1) * (r * jax.nn.sigmoid(r))

    u = u_ref[...]
    H = POOL_HALO
    ext_ref[0:H, :] = jnp.where(is_first, 0.0, uprev_ref[...])
    ext_ref[H:H + tm, :] = u
    ext_ref[H + tm:2 * H + tm, :] = jnp.where(is_last, 0.0, unext_ref[...])
    seq_len = jnp.where(in_ctx, n_ctx, n_lat)
    t = (i - jnp.where(in_ctx, lat_tiles, 0)) * tm + lax.broadcasted_iota(jnp.int32, (tm, 1), 0)
    pooled = []
    for gi, w in enumerate(POOL_WINDOWS):
        cols = slice(gi * POOL_GC, (gi + 1) * POOL_GC)
        s = ext_ref[H - w // 2:H - w // 2 + tm, cols]
        for dlt in range(-w // 2 + 1, w - w // 2):
            s = s + ext_ref[H + dlt:H + dlt + tm, cols]
        lo = jnp.maximum(t - w // 2, 0)
        hi = jnp.minimum(t + (w - w // 2), seq_len)
        cnt = (hi - lo).astype(F32)
        pg = (s / cnt - u[:, cols]).astype(BF16)
        pooled.append(jnp.dot(pg, pw_ref[gi], preferred_element_type=F32) * ps_ref[:, cols])

    mix = jnp.concatenate([gl] + pooled, axis=1).astype(BF16)
    y = jnp.dot(mix, wout_ref[...], preferred_element_type=F32)
    is_ctx = _is_ctx_rows(i * tm, tm, n_lat)
    o_ref[...] = h_ref[...] + _mod_row(mod_ref, is_ctx, 2) * y


def _mix0_out(h, mod3, o_f, o_b, p0, norm_w, pool_w, pool_scale, w_out, *, n_lat, n_ctx):
    rows, d = h.shape
    tm = 256
    gw = GLA_HEADS * GLA_DV
    pwid = len(POOL_WINDOWS) * POOL_GC
    hb = tm // POOL_HALO
    n_hblk = rows // POOL_HALO
    r_col = (2 * GLA_HEADS * GLA_DK + gw) // gw
    u_col = (2 * GLA_HEADS * GLA_DK + 2 * gw) // pwid
    return pl.pallas_call(
        functools.partial(_mix0_out_kernel, n_lat=n_lat, n_ctx=n_ctx, tm=tm),
        grid=(rows // tm,),
        in_specs=[pl.BlockSpec((tm, d), lambda i: (i, 0)),
                  pl.BlockSpec((2, 3, d), lambda i: (0, 0, 0)),
                  pl.BlockSpec((tm, gw), lambda i: (i, 0)),
                  pl.BlockSpec((tm, gw), lambda i: (i, 0)),
                  pl.BlockSpec((tm, gw), lambda i: (i, r_col)),
                  pl.BlockSpec((POOL_HALO, pwid), lambda i: (jnp.maximum(i * hb - 1, 0), u_col)),
                  pl.BlockSpec((tm, pwid), lambda i: (i, u_col)),
                  pl.BlockSpec((POOL_HALO, pwid),
                               lambda i: (jnp.minimum((i + 1) * hb, n_hblk - 1), u_col)),
                  pl.BlockSpec((1, GLA_DV), lambda i: (0, 0)),
                  pl.BlockSpec((len(POOL_WINDOWS), POOL_GC, POOL_GC), lambda i: (0, 0, 0)),
                  pl.BlockSpec((1, pwid), lambda i: (0, 0)),
                  pl.BlockSpec((d, d), lambda i: (0, 0))],
        out_specs=pl.BlockSpec((tm, d), lambda i: (i, 0)),
        out_shape=jax.ShapeDtypeStruct((rows, d), F32),
        scratch_shapes=[pltpu.VMEM((tm + 2 * POOL_HALO, pwid), F32)],
        compiler_params=_cparams(("arbitrary",), 48),
        name="mix0_readout",
    )(h, mod3, o_f, o_b, p0, p0, p0, p0, norm_w.reshape(1, GLA_DV), pool_w,
      pool_scale.reshape(1, pwid), w_out)


def _proj1_kernel(h_ref, mod_ref, w_ref, cos_ref, sin_ref, o_ref, z_ref,
                  *, n_lat, tm, tn, n_qt, qscale):
    i = pl.program_id(0)
    j = pl.program_id(1)

    @pl.when(j == 0)
    def _():
        is_ctx = _is_ctx_rows(i * tm, tm, n_lat)
        z_ref[...] = _modulated(h_ref[...], mod_ref, is_ctx).astype(BF16)

    acc = jnp.dot(z_ref[...], w_ref[...], preferred_element_type=F32)

    @pl.when(j < 2 * n_qt)
    def _():
        scale = jnp.where(j < n_qt, qscale, 1.0)
        cs = cos_ref[...] * scale
        sn = sin_ref[...] * scale
        for gidx in range(tn // LANES):
            x = acc[:, gidx * LANES:(gidx + 1) * LANES]
            y = x * cs + pltpu.roll(x, LANES // 2, 1) * sn
            o_ref[:, gidx * LANES:(gidx + 1) * LANES] = y.astype(BF16)

    @pl.when(j >= 2 * n_qt)
    def _():
        o_ref[...] = acc.astype(BF16)


def _proj1(h, mod3, w_qkv, cos_t, sin_t, *, tm, n_lat):
    rows, d = h.shape
    n = w_qkv.shape[1]
    tn = 512
    qscale = (DIFF_DH ** -0.5) * math.log2(math.e)
    return pl.pallas_call(
        functools.partial(_proj1_kernel, n_lat=n_lat, tm=tm, tn=tn, n_qt=d // tn, qscale=qscale),
        grid=(rows // tm, n // tn),
        in_specs=[pl.BlockSpec((tm, d), lambda i, j: (i, 0)),
                  pl.BlockSpec((2, 3, d), lambda i, j: (0, 0, 0)),
                  pl.BlockSpec((d, tn), lambda i, j: (0, j)),
                  pl.BlockSpec((tm, LANES), lambda i, j: (i, 0)),
                  pl.BlockSpec((tm, LANES), lambda i, j: (i, 0))],
        out_specs=pl.BlockSpec((tm, tn), lambda i, j: (i, j)),
        out_shape=jax.ShapeDtypeStruct((rows, n), BF16),
        scratch_shapes=[pltpu.VMEM((tm, d), BF16)],
        compiler_params=_cparams(("arbitrary", "arbitrary"), 40),
        name="proj_qkv_rope",
    )(h, mod3, w_qkv, cos_t, sin_t)


def _attn_kernel(lam_ref, q_ref, k_ref, v_ref, nw_ref, o_ref, m_ref, l_ref, acc_ref,
                 *, tk, n_kv, lam_init):
    m_ref[...] = jnp.full_like(m_ref, -jnp.inf)
    l_ref[...] = jnp.zeros_like(l_ref)
    acc_ref[...] = jnp.zeros_like(acc_ref)

    def body(jj, carry):
        start = pl.multiple_of(jj * tk, tk)
        kj = k_ref[pl.ds(start, tk), :]
        vj = v_ref[pl.ds(start, tk), :]
        for c in range(2):
            qc = q_ref[:, c * DIFF_DH:(c + 1) * DIFF_DH]
            s = lax.dot_general(qc, kj[:, c * DIFF_DH:(c + 1) * DIFF_DH],
                                (((1,), (1,)), ((), ())), preferred_element_type=F32)
            m_old = m_ref[c]
            m_new = jnp.maximum(m_old, jnp.max(s, axis=-1, keepdims=True))
            alpha = jnp.exp2(m_old - m_new)
            p = jnp.exp2(s - m_new)
            l_ref[c] = alpha * l_ref[c] + jnp.sum(p, axis=-1, keepdims=True)
            acc_ref[c] = alpha * acc_ref[c] + jnp.dot(p.astype(BF16), vj,
                                                      preferred_element_type=F32)
            m_ref[c] = m_new
        return carry

    lax.fori_loop(0, n_kv, body, 0)

    lp = lam_ref[...]
    lam = (jnp.exp(jnp.sum(lp[0:1] * lp[1:2], axis=-1, keepdims=True))
           - jnp.exp(jnp.sum(lp[2:3] * lp[3:4], axis=-1, keepdims=True)) + lam_init)
    o = acc_ref[0] / l_ref[0] - lam * (acc_ref[1] / l_ref[1])
    o_ref[...] = (_rms(o) * nw_ref[...] * (1.0 - lam_init)).astype(BF16)


def _diff_attn(qkv, lam_p, norm_w, *, n_lat, lam_init):
    rows = qkv.shape[0]
    d = DIFF_HEADS * DIFF_DV
    tq = 512
    tk = 768
    return pl.pallas_call(
        functools.partial(_attn_kernel, tk=tk, n_kv=rows // tk, lam_init=lam_init),
        grid=(DIFF_HEADS, n_lat // tq),
        in_specs=[pl.BlockSpec((4, DIFF_DH), lambda h, i: (0, 0)),
                  pl.BlockSpec((tq, DIFF_DV), lambda h, i: (i, h)),
                  pl.BlockSpec((rows, DIFF_DV), lambda h, i: (0, DIFF_HEADS + h)),
                  pl.BlockSpec((rows, DIFF_DV), lambda h, i: (0, 2 * DIFF_HEADS + h)),
                  pl.BlockSpec((1, DIFF_DV), lambda h, i: (0, 0))],
        out_specs=pl.BlockSpec((tq, DIFF_DV), lambda h, i: (i, h)),
        out_shape=jax.ShapeDtypeStruct((n_lat, d), BF16),
        scratch_shapes=[pltpu.VMEM((2, tq, 1), F32), pltpu.VMEM((2, tq, 1), F32),
                        pltpu.VMEM((2, tq, DIFF_DV), F32)],
        compiler_params=_cparams(("arbitrary", "arbitrary"), 48),
        name="diff_attn",
    )(lam_p, qkv, qkv, qkv, norm_w.reshape(1, DIFF_DV))


def _outproj_kernel(h_ref, mod_ref, x_ref, w_ref, o_ref):
    y = jnp.dot(x_ref[...], w_ref[...], preferred_element_type=F32)
    o_ref[...] = h_ref[...] + mod_ref[0, 2:3, :] * y


def _outproj_latent(h, mod3, x, w, *, n_lat):
    d = h.shape[1]
    tm = 512
    return pl.pallas_call(
        _outproj_kernel,
        grid=(n_lat // tm,),
        in_specs=[pl.BlockSpec((tm, d), lambda i: (i, 0)),
                  pl.BlockSpec((2, 3, d), lambda i: (0, 0, 0)),
                  pl.BlockSpec((tm, d), lambda i: (i, 0)),
                  pl.BlockSpec((d, d), lambda i: (0, 0))],
        out_specs=pl.BlockSpec((tm, d), lambda i: (i, 0)),
        out_shape=jax.ShapeDtypeStruct((n_lat, d), F32),
        compiler_params=_cparams(("arbitrary",), 40),
        name="attn_outproj",
    )(h, mod3, x, w)


def _rope_tables(n_lat, n_ctx):
    rows = n_lat // GRID_W
    row = jnp.repeat(jnp.arange(rows), GRID_W).astype(F32)
    col = jnp.tile(jnp.arange(GRID_W), rows).astype(F32)
    n_freq = DIFF_DH // 4
    inv_freq = ROPE_THETA ** (-jnp.arange(n_freq, dtype=F32) / n_freq)
    ang = jnp.concatenate([row[:, None] * inv_freq, col[:, None] * inv_freq], axis=-1)
    cos, sin = jnp.cos(ang), jnp.sin(ang)
    cos_t = jnp.concatenate([cos, cos], axis=-1)
    sin_t = jnp.concatenate([-sin, sin], axis=-1)
    cos_t = jnp.concatenate([cos_t, jnp.ones((n_ctx, DIFF_DH), F32)], axis=0)
    sin_t = jnp.concatenate([sin_t, jnp.zeros((n_ctx, DIFF_DH), F32)], axis=0)
    return cos_t, sin_t


def kernel(x, c, ctx, c_ctx, ada_w, ada_b, ffn_w1, ffn_w3, ffn_w2, gla_w_in, gla_gate_w2,
           gla_gate_b, gla_norm_w, pool_w, pool_scale, mix0_w_out, diff_w_qkv, diff_lambda,
           diff_norm_w, diff_w_out, final_norm_w):
    assert x.shape[0] == 1 and ada_w.shape[0] == 2
    n_lat, d = x.shape[1], x.shape[2]
    n_ctx = ctx.shape[1]
    rows = n_lat + n_ctx
    tm = 768
    assert rows % tm == 0 and n_lat % 512 == 0 and n_ctx % 256 == 0 and n_lat % GRID_W == 0

    h = jnp.concatenate([x[0], ctx[0]], axis=0)
    c8 = jnp.zeros((8, d), F32).at[0].set(c[0]).at[1].set(c_ctx)
    mods = _ada(c8, ada_w, ada_b)[:, :2].reshape(2, 2, 9, d)

    w1 = ffn_w1.astype(BF16)
    w3 = ffn_w3.astype(BF16)
    w2 = ffn_w2.astype(BF16)

    m = mods[0]
    h = _ffn(h, m[:, 0:3], w1[0, 0], w3[0, 0], w2[0, 0], rows=rows, tm=tm, n_lat=n_lat)
    w_in = gla_w_in[0]
    n_qkvr = 2 * GLA_HEADS * GLA_DK + 2 * GLA_HEADS * GLA_DV
    n_gate = 2 * GLA_GATE_RANK
    w_main = jnp.concatenate([w_in[:, :n_qkvr], w_in[:, n_qkvr + n_gate:]], axis=1).astype(BF16)
    w_gate = jnp.pad(w_in[:, n_qkvr:n_qkvr + n_gate], ((0, 0), (0, LANES - n_gate))).astype(BF16)
    gw2 = gla_gate_w2[0].astype(BF16)
    gw2p = jnp.zeros((2, LANES, gw2.shape[-1]), BF16)
    gw2p = gw2p.at[0, :GLA_GATE_RANK].set(gw2[0]).at[1, GLA_GATE_RANK:n_gate].set(gw2[1])
    p0, lf, lb = _proj0(h, m[:, 3:6], w_main, w_gate, gw2p, gla_gate_b[0], tm=tm, n_lat=n_lat)
    v_lo = 2 * GLA_HEADS * GLA_DK
    v16 = p0[:, v_lo:v_lo + GLA_HEADS * GLA_DV].astype(BF16)
    o_f, o_b = _gla(p0, v16, lf, lb, n_lat=n_lat)
    h = _mix0_out(h, m[:, 3:6], o_f, o_b, p0, gla_norm_w[0], pool_w[0].astype(BF16),
                  pool_scale[0], mix0_w_out[0].astype(BF16), n_lat=n_lat, n_ctx=n_ctx)
    h = _ffn(h, m[:, 6:9], w1[0, 1], w3[0, 1], w2[0, 1], rows=rows, tm=tm, n_lat=n_lat)

    m = mods[1]
    h = _ffn(h, m[:, 0:3], w1[1, 0], w3[1, 0], w2[1, 0], rows=rows, tm=tm, n_lat=n_lat)
    cos_t, sin_t = _rope_tables(n_lat, n_ctx)
    qkv = _proj1(h, m[:, 3:6], diff_w_qkv[0].astype(BF16), cos_t, sin_t, tm=tm, n_lat=n_lat)
    lam_init = 0.8 - 0.6 * math.exp(-0.3 * 1)
    o = _diff_attn(qkv, diff_lambda[0], diff_norm_w[0], n_lat=n_lat, lam_init=lam_init)
    hl = _outproj_latent(h, m[:, 3:6], o, diff_w_out[0].astype(BF16), n_lat=n_lat)
    out = _ffn(hl, m[:, 6:9], w1[1, 1], w3[1, 1], w2[1, 1], rows=n_lat, tm=512, n_lat=n_lat,
               final_w=final_norm_w)
    return out[None]
```

```python
import functools
import math

import jax
import jax.numpy as jnp
from jax import lax
from jax.experimental import pallas as pl
from jax.experimental.pallas import tpu as pltpu

F32 = jnp.float32
BF16 = jnp.bfloat16

NORM_EPS = 1e-6
GRID_W = 64
ROPE_THETA = 10000.0

GLA_HEADS = 4
GLA_DK = 128
GLA_DV = 256
GLA_GATE_RANK = 16
GLA_TAU = 16.0
GLA_CHUNK = 64
GLA_SUB = 16
GLA_GROUP = 256
POOL_WINDOWS = (2, 4, 8, 16)
POOL_GC = 256
POOL_HALO = 8

DIFF_HEADS = 8
DIFF_DH = 128
DIFF_DV = 256

LANES = 128
MIB = 1024 * 1024


def _cparams(semantics, vmem_mib):
    return pltpu.CompilerParams(dimension_semantics=semantics,
                                vmem_limit_bytes=int(vmem_mib * MIB))


def _rms(x):
    return x * lax.rsqrt(jnp.mean(x * x, axis=-1, keepdims=True) + NORM_EPS)


def _is_ctx_rows(row0, rows, n_lat):
    return (row0 + lax.broadcasted_iota(jnp.int32, (rows, 1), 0)) >= n_lat


def _mod_row(mod_ref, is_ctx, k):
    return jnp.where(is_ctx, mod_ref[1, k:k + 1, :], mod_ref[0, k:k + 1, :])


def _modulated(h, mod_ref, is_ctx):
    return _rms(h) * (1.0 + _mod_row(mod_ref, is_ctx, 1)) + _mod_row(mod_ref, is_ctx, 0)


def _ada_kernel(c_ref, w_ref, b_ref, o_ref):
    c = c_ref[...]
    s = (c * jax.nn.sigmoid(c)).astype(BF16)
    o_ref[0] = jnp.dot(s, w_ref[0].astype(BF16), preferred_element_type=F32) + b_ref[0]


def _ada(c8, ada_w, ada_b):
    depth, d, n = ada_w.shape
    tn = 1024
    return pl.pallas_call(
        _ada_kernel,
        grid=(depth, n // tn),
        in_specs=[pl.BlockSpec((8, d), lambda i, j: (0, 0)),
                  pl.BlockSpec((1, d, tn), lambda i, j: (i, 0, j)),
                  pl.BlockSpec((1, 1, tn), lambda i, j: (i, 0, j))],
        out_specs=pl.BlockSpec((1, 8, tn), lambda i, j: (i, 0, j)),
        out_shape=jax.ShapeDtypeStruct((depth, 8, n), F32),
        compiler_params=_cparams(("arbitrary", "arbitrary"), 40),
        name="ada_mod",
    )(c8, ada_w, ada_b.reshape(depth, 1, n))


def _ffn_kernel(*refs, n_lat, tm, final):
    if final:
        h_ref, mod_ref, w1_ref, w3_ref, w2_ref, fnw_ref, o_ref, z_ref = refs
    else:
        h_ref, mod_ref, w1_ref, w3_ref, w2_ref, o_ref, z_ref = refs
    i = pl.program_id(0)
    j = pl.program_id(1)
    is_ctx = _is_ctx_rows(i * tm, tm, n_lat)

    @pl.when(j == 0)
    def _():
        z_ref[...] = _modulated(h_ref[...], mod_ref, is_ctx).astype(BF16)
        o_ref[...] = jnp.zeros_like(o_ref)

    z = z_ref[...]
    u = jnp.dot(z, w1_ref[...], preferred_element_type=F32)
    g = jnp.dot(z, w3_ref[...], preferred_element_type=F32)
    a = (u * jax.nn.sigmoid(u) * g).astype(BF16)
    o_ref[...] += jnp.dot(a, w2_ref[...], preferred_element_type=F32)

    @pl.when(j == pl.num_programs(1) - 1)
    def _():
        hn = h_ref[...] + 0.5 * _mod_row(mod_ref, is_ctx, 2) * o_ref[...]
        if final:
            hn = _rms(hn) * fnw_ref[...]
        o_ref[...] = hn


def _ffn(h, mod3, w1, w3, w2, layer, half, *, rows, tm, n_lat, final_w=None):
    d = h.shape[1]
    dff = w1.shape[-1]
    tf = 512
    final = final_w is not None
    in_specs = [pl.BlockSpec((tm, d), lambda i, j: (i, 0)),
                pl.BlockSpec((2, 3, d), lambda i, j: (0, 0, 0)),
                pl.BlockSpec((None, None, d, tf), lambda i, j: (layer, half, 0, j)),
                pl.BlockSpec((None, None, d, tf), lambda i, j: (layer, half, 0, j)),
                pl.BlockSpec((None, None, tf, d), lambda i, j: (layer, half, j, 0))]
    args = [h, mod3, w1, w3, w2]
    if final:
        in_specs.append(pl.BlockSpec((1, d), lambda i, j: (0, 0)))
        args.append(final_w.reshape(1, d))
    return pl.pallas_call(
        functools.partial(_ffn_kernel, n_lat=n_lat, tm=tm, final=final),
        grid=(rows // tm, dff // tf),
        in_specs=in_specs,
        out_specs=pl.BlockSpec((tm, d), lambda i, j: (i, 0)),
        out_shape=jax.ShapeDtypeStruct((rows, d), F32),
        scratch_shapes=[pltpu.VMEM((tm, d), BF16)],
        compiler_params=_cparams(("arbitrary", "arbitrary"), 56),
        name="ffn_final" if final else "ffn_half",
    )(*args)


def _log_sigmoid(x):
    return jnp.minimum(x, 0.0) - jnp.log1p(jnp.exp(-jnp.abs(x)))


def _proj0_kernel(h_ref, mod_ref, w_ref, wg_ref, gw2_ref, gb_ref,
                  p_ref, lf_ref, lb_ref, z_ref, *, n_lat, tm):
    i = pl.program_id(0)
    j = pl.program_id(1)

    @pl.when(j == 0)
    def _():
        is_ctx = _is_ctx_rows(i * tm, tm, n_lat)
        z = _modulated(h_ref[...], mod_ref, is_ctx).astype(BF16)
        z_ref[...] = z
        gz = jnp.dot(z, wg_ref[...], preferred_element_type=F32).astype(BF16)
        for d, out in ((0, lf_ref), (1, lb_ref)):
            zz = jnp.dot(gz, gw2_ref[d], preferred_element_type=F32) + gb_ref[d]
            out[...] = _log_sigmoid(zz) / GLA_TAU

    p_ref[...] = jnp.dot(z_ref[...], w_ref[...], preferred_element_type=F32)


def _proj0(h, mod3, w_main, w_gate, gate_w2p, gate_b, *, tm, n_lat):
    rows, d = h.shape
    n = w_main.shape[1]
    qk = gate_b.shape[-1]
    tn = 512
    return pl.pallas_call(
        functools.partial(_proj0_kernel, n_lat=n_lat, tm=tm),
        grid=(rows // tm, n // tn),
        in_specs=[pl.BlockSpec((tm, d), lambda i, j: (i, 0)),
                  pl.BlockSpec((2, 3, d), lambda i, j: (0, 0, 0)),
                  pl.BlockSpec((d, tn), lambda i, j: (0, j)),
                  pl.BlockSpec((d, LANES), lambda i, j: (0, 0)),
                  pl.BlockSpec((2, LANES, qk), lambda i, j: (0, 0, 0)),
                  pl.BlockSpec((2, 1, qk), lambda i, j: (0, 0, 0))],
        out_specs=[pl.BlockSpec((tm, tn), lambda i, j: (i, j)),
                   pl.BlockSpec((tm, qk), lambda i, j: (i, 0)),
                   pl.BlockSpec((tm, qk), lambda i, j: (i, 0))],
        out_shape=[jax.ShapeDtypeStruct((rows, n), F32),
                   jax.ShapeDtypeStruct((rows, qk), F32),
                   jax.ShapeDtypeStruct((rows, qk), F32)],
        scratch_shapes=[pltpu.VMEM((tm, d), BF16)],
        compiler_params=_cparams(("arbitrary", "arbitrary"), 48),
        name="proj_gla_pool",
    )(h, mod3, w_main, w_gate, gate_w2p, gate_b.reshape(2, 1, qk))


def _split3(x):
    x1 = x.astype(BF16)
    r1 = x - x1.astype(F32)
    x2 = r1.astype(BF16)
    x3 = (r1 - x2.astype(F32)).astype(BF16)
    return x1, x2, x3


def _gla_chunk(q_ref, k_ref, v_ref, g_ref, o_ref, s_ref, b_ref, base, reverse):
    C, SB = GLA_CHUNK, GLA_SUB
    nb = C // SB
    rows = pl.ds(base, C)
    q = q_ref[rows, :] * (GLA_DK ** -0.5)
    k = k_ref[rows, :]
    v = v_ref[rows, :]
    g = g_ref[rows, :]

    ti = lax.broadcasted_iota(jnp.int32, (C, C), 0)
    si = lax.broadcasted_iota(jnp.int32, (C, C), 1)
    tri = jnp.where((si >= ti) if reverse else (si <= ti), 1.0, 0.0).astype(BF16)
    b = sum(jnp.dot(tri, part, preferred_element_type=F32) for part in _split3(g))
    b_ref[...] = b
    s_old = s_ref[...]

    inter = jnp.dot((q * jnp.exp(b)).astype(BF16), s_old.astype(BF16),
                    preferred_element_type=F32)

    col = lax.broadcasted_iota(jnp.int32, (SB, C), 1)
    lane = lax.broadcasted_iota(jnp.int32, (SB, LANES), 1)
    trow = lax.broadcasted_iota(jnp.int32, (SB, LANES), 0)
    ones = jnp.ones((LANES, LANES), BF16)
    a_rows = []
    for i in range(nb):
        blk = slice(i * SB, (i + 1) * SB)
        q_i = q[blk]
        b_i = b[blk]
        edge = (i + 1) * SB if reverse else i * SB - 1
        has_off = (i < nb - 1) if reverse else (i > 0)
        if has_off:
            r_i = b_ref[edge:edge + 1, :]
            qe = (q_i * jnp.exp(b_i - r_i)).astype(BF16)
            ke = (k * jnp.exp(jnp.minimum(r_i - b, 0.0))).astype(BF16)
            raw = lax.dot_general(qe, ke, (((1,), (1,)), ((), ())),
                                  preferred_element_type=F32)
            off_mask = (col >= (i + 1) * SB) if reverse else (col < i * SB)
            a_i = jnp.where(off_mask, raw, 0.0)
        else:
            a_i = jnp.zeros((SB, C), F32)
        terms = []
        for s in range(SB):
            row = base + i * SB + s
            k_s = k_ref[pl.ds(row, 1), :]
            b_s = b_ref[i * SB + s:i * SB + s + 1, :]
            keep = (trow <= s) if reverse else (trow >= s)
            terms.append(jnp.where(keep, q_i * k_s * jnp.exp(b_i - b_s), 0.0))
        stacked = jnp.concatenate(terms, axis=0).astype(BF16)
        dsum = jnp.dot(stacked, ones, preferred_element_type=F32)
        diag = jnp.zeros((SB, LANES), F32)
        for s in range(SB):
            diag = diag + jnp.where(lane == i * SB + s, dsum[s * SB:(s + 1) * SB], 0.0)
        a_rows.append(a_i + diag[:, :C])
    attn = jnp.concatenate(a_rows, axis=0).astype(BF16)
    intra = jnp.dot(attn, v, preferred_element_type=F32)
    o_ref[rows, :] = inter + intra

    last = 0 if reverse else C - 1
    b_last = b_ref[last:last + 1, :]
    ke = (k * jnp.exp(b_last - b)).astype(BF16)
    upd = lax.dot_general(ke, v, (((0,), (0,)), ((), ())), preferred_element_type=F32)
    decay = jnp.broadcast_to(jnp.exp(b_last), (GLA_DK, GLA_DK)).T
    s_ref[...] = jnp.concatenate([decay, decay], axis=1) * s_old + upd


def _gla_kernel(qf_ref, kf_ref, vf_ref, gf_ref, qb_ref, kb_ref, vb_ref, gb_ref,
                of_ref, ob_ref, sf_ref, sb_ref, bf_ref, bb_ref):
    @pl.when(pl.program_id(1) == 0)
    def _():
        sf_ref[...] = jnp.zeros_like(sf_ref)
        sb_ref[...] = jnp.zeros_like(sb_ref)

    n_sub = GLA_GROUP // GLA_CHUNK

    def body(c, carry):
        fbase = pl.multiple_of(c * GLA_CHUNK, GLA_CHUNK)
        rbase = pl.multiple_of((n_sub - 1 - c) * GLA_CHUNK, GLA_CHUNK)
        _gla_chunk(qf_ref, kf_ref, vf_ref, gf_ref, of_ref, sf_ref, bf_ref, fbase, False)
        _gla_chunk(qb_ref, kb_ref, vb_ref, gb_ref, ob_ref, sb_ref, bb_ref, rbase, True)
        return carry

    lax.fori_loop(0, n_sub, body, 0)


def _gla(p0, v16, lf, lb, *, n_lat):
    rows = p0.shape[0]
    n_grp = rows // GLA_GROUP
    lat_grp = n_lat // GLA_GROUP
    G = GLA_GROUP

    def fwd(c):
        return (c + lat_grp) % n_grp

    def bwd(c):
        return n_grp - 1 - c

    def spec(width, order, col):
        return pl.BlockSpec((G, width), lambda h, c: (order(c), col(h)))

    in_specs = []
    for order in (fwd, bwd):
        in_specs += [spec(GLA_DK, order, lambda h: h),
                     spec(GLA_DK, order, lambda h: GLA_HEADS + h),
                     spec(GLA_DV, order, lambda h: h),
                     spec(GLA_DK, order, lambda h: h)]
    out_specs = [spec(GLA_DV, fwd, lambda h: h), spec(GLA_DV, bwd, lambda h: h)]
    width = GLA_HEADS * GLA_DV
    return pl.pallas_call(
        _gla_kernel,
        grid=(GLA_HEADS, n_grp),
        in_specs=in_specs,
        out_specs=out_specs,
        out_shape=[jax.ShapeDtypeStruct((rows, width), F32)] * 2,
        scratch_shapes=[pltpu.VMEM((GLA_DK, GLA_DV), F32), pltpu.VMEM((GLA_DK, GLA_DV), F32),
                        pltpu.VMEM((GLA_CHUNK, GLA_DK), F32), pltpu.VMEM((GLA_CHUNK, GLA_DK), F32)],
        compiler_params=_cparams(("arbitrary", "arbitrary"), 32),
        name="gla_scan",
    )(p0, p0, v16, lf, p0, p0, v16, lb)


def _mix0_out_kernel(h_ref, mod_ref, of_ref, ob_ref, r_ref, uprev_ref, u_ref, unext_ref,
                     nw_ref, pw_ref, ps_ref, wout_ref, o_ref, ext_ref,
                     *, n_lat, n_ctx, tm):
    i = pl.program_id(0)
    lat_tiles = n_lat // tm
    n_tiles = (n_lat + n_ctx) // tm
    in_ctx = i >= lat_tiles
    is_first = (i == 0) | (i == lat_tiles)
    is_last = (i == lat_tiles - 1) | (i == n_tiles - 1)

    o = of_ref[...] + ob_ref[...]
    r = r_ref[...]
    heads = []
    for hd in range(GLA_HEADS):
        heads.append(_rms(o[:, hd * GLA_DV:(hd + 1) * GLA_DV]) * nw_ref[...])
    gl = jnp.concatenate(heads, axis=1) * (r * jax.nn.sigmoid(r))

    u = u_ref[...]
    H = POOL_HALO
    ext_ref[0:H, :] = jnp.where(is_first, 0.0, uprev_ref[...])
    ext_ref[H:H + tm, :] = u
    ext_ref[H + tm:2 * H + tm, :] = jnp.where(is_last, 0.0, unext_ref[...])
    seq_len = jnp.where(in_ctx, n_ctx, n_lat)
    t = (i - jnp.where(in_ctx, lat_tiles, 0)) * tm + lax.broadcasted_iota(jnp.int32, (tm, 1), 0)
    pooled = []
    for gi, w in enumerate(POOL_WINDOWS):
        cols = slice(gi * POOL_GC, (gi + 1) * POOL_GC)
        s = ext_ref[H - w // 2:H - w // 2 + tm, cols]
        for dlt in range(-w // 2 + 1, w - w // 2):
            s = s + ext_ref[H + dlt:H + dlt + tm, cols]
        lo = jnp.maximum(t - w // 2, 0)
        hi = jnp.minimum(t + (w - w // 2), seq_len)
        cnt = (hi - lo).astype(F32)
        pg = (s / cnt - u[:, cols]).astype(BF16)
        pooled.append(jnp.dot(pg, pw_ref[gi], preferred_element_type=F32) * ps_ref[:, cols])

    mix = jnp.concatenate([gl] + pooled, axis=1).astype(BF16)
    y = jnp.dot(mix, wout_ref[...], preferred_element_type=F32)
    is_ctx = _is_ctx_rows(i * tm, tm, n_lat)
    o_ref[...] = h_ref[...] + _mod_row(mod_ref, is_ctx, 2) * y


def _mix0_out(h, mod3, o_f, o_b, p0, norm_w, pool_w, pool_scale, w_out, *, n_lat, n_ctx):
    rows, d = h.shape
    tm = 256
    gw = GLA_HEADS * GLA_DV
    pwid = len(POOL_WINDOWS) * POOL_GC
    hb = tm // POOL_HALO
    n_hblk = rows // POOL_HALO
    r_col = (2 * GLA_HEADS * GLA_DK + gw) // gw
    u_col = (2 * GLA_HEADS * GLA_DK + 2 * gw) // pwid
    return pl.pallas_call(
        functools.partial(_mix0_out_kernel, n_lat=n_lat, n_ctx=n_ctx, tm=tm),
        grid=(rows // tm,),
        in_specs=[pl.BlockSpec((tm, d), lambda i: (i, 0)),
                  pl.BlockSpec((2, 3, d), lambda i: (0, 0, 0)),
                  pl.BlockSpec((tm, gw), lambda i: (i, 0)),
                  pl.BlockSpec((tm, gw), lambda i: (i, 0)),
                  pl.BlockSpec((tm, gw), lambda i: (i, r_col)),
                  pl.BlockSpec((POOL_HALO, pwid), lambda i: (jnp.maximum(i * hb - 1, 0), u_col)),
                  pl.BlockSpec((tm, pwid), lambda i: (i, u_col)),
                  pl.BlockSpec((POOL_HALO, pwid),
                               lambda i: (jnp.minimum((i + 1) * hb, n_hblk - 1), u_col)),
                  pl.BlockSpec((1, GLA_DV), lambda i: (0, 0)),
                  pl.BlockSpec((len(POOL_WINDOWS), POOL_GC, POOL_GC), lambda i: (0, 0, 0)),
                  pl.BlockSpec((1, pwid), lambda i: (0, 0)),
                  pl.BlockSpec((d, d), lambda i: (0, 0))],
        out_specs=pl.BlockSpec((tm, d), lambda i: (i, 0)),
        out_shape=jax.ShapeDtypeStruct((rows, d), F32),
        scratch_shapes=[pltpu.VMEM((tm + 2 * POOL_HALO, pwid), F32)],
        compiler_params=_cparams(("arbitrary",), 48),
        name="mix0_readout",
    )(h, mod3, o_f, o_b, p0, p0, p0, p0, norm_w.reshape(1, GLA_DV), pool_w,
      pool_scale.reshape(1, pwid), w_out)


def _proj1_kernel(h_ref, mod_ref, wqk_ref, wvt_ref, cos_ref, sin_ref, qk_ref, vt_ref, z_ref,
                  *, n_lat, tm, tn, n_qt, qscale):
    i = pl.program_id(0)
    j = pl.program_id(1)

    @pl.when(j == 0)
    def _():
        is_ctx = _is_ctx_rows(i * tm, tm, n_lat)
        z_ref[...] = _modulated(h_ref[...], mod_ref, is_ctx).astype(BF16)

    @pl.when(j < 2 * n_qt)
    def _():
        acc = jnp.dot(z_ref[...], wqk_ref[...], preferred_element_type=F32)
        scale = jnp.where(j < n_qt, qscale, 1.0)
        cs = cos_ref[...] * scale
        sn = sin_ref[...] * scale
        for gidx in range(tn // LANES):
            x = acc[:, gidx * LANES:(gidx + 1) * LANES]
            y = x * cs + pltpu.roll(x, LANES // 2, 1) * sn
            qk_ref[:, gidx * LANES:(gidx + 1) * LANES] = y.astype(BF16)

    @pl.when(j >= 2 * n_qt)
    def _():
        vt = lax.dot_general(wvt_ref[...], z_ref[...], (((1,), (1,)), ((), ())),
                             preferred_element_type=F32)
        for hd in range(tn // DIFF_DV):
            vt_ref[hd] = vt[hd * DIFF_DV:(hd + 1) * DIFF_DV].astype(BF16)


def _proj1(h, mod3, w_qk, w_vt, cos_t, sin_t, *, tm, n_lat):
    rows, d = h.shape
    tn = 512
    n_qt = d // tn
    n_vt = w_vt.shape[0] // tn
    hpt = tn // DIFF_DV
    qscale = (DIFF_DH ** -0.5) * math.log2(math.e)
    return pl.pallas_call(
        functools.partial(_proj1_kernel, n_lat=n_lat, tm=tm, tn=tn, n_qt=n_qt, qscale=qscale),
        grid=(rows // tm, 2 * n_qt + n_vt),
        in_specs=[pl.BlockSpec((tm, d), lambda i, j: (i, 0)),
                  pl.BlockSpec((2, 3, d), lambda i, j: (0, 0, 0)),
                  pl.BlockSpec((d, tn), lambda i, j: (0, jnp.minimum(j, 2 * n_qt - 1))),
                  pl.BlockSpec((tn, d), lambda i, j: (jnp.maximum(j - 2 * n_qt, 0), 0)),
                  pl.BlockSpec((tm, LANES), lambda i, j: (i, 0)),
                  pl.BlockSpec((tm, LANES), lambda i, j: (i, 0))],
        out_specs=[pl.BlockSpec((tm, tn), lambda i, j: (i, jnp.minimum(j, 2 * n_qt - 1))),
                   pl.BlockSpec((hpt, None, DIFF_DV, tm),
                                lambda i, j: (jnp.maximum(j - 2 * n_qt, 0), i, 0, 0))],
        out_shape=[jax.ShapeDtypeStruct((rows, 2 * d), BF16),
                   jax.ShapeDtypeStruct((DIFF_HEADS, rows // tm, DIFF_DV, tm), BF16)],
        scratch_shapes=[pltpu.VMEM((tm, d), BF16)],
        compiler_params=_cparams(("arbitrary", "arbitrary"), 40),
        name="proj_qkv_rope",
    )(h, mod3, w_qk, w_vt, cos_t, sin_t)


def _attn_kernel(lam_ref, q_ref, k_ref, vt_ref, nw_ref, o_ref, m_ref, l_ref, acc_ref,
                 sa_ref, sb_ref, ca_ref, cb_ref, *, tk, n_kv, lam_init):
    m_ref[...] = jnp.full_like(m_ref, -jnp.inf)
    l_ref[...] = jnp.zeros_like(l_ref)
    acc_ref[...] = jnp.zeros_like(acc_ref)

    def scores(jj, s_ref, c_ref):
        start = pl.multiple_of(jj * tk, tk)
        for c in range(2):
            kc = k_ref[pl.ds(start, tk), c * DIFF_DH:(c + 1) * DIFF_DH]
            qc = q_ref[:, c * DIFF_DH:(c + 1) * DIFF_DH]
            st = lax.dot_general(kc, qc, (((1,), (1,)), ((), ())), preferred_element_type=F32)
            s_ref[c] = st
            c_ref[c] = jnp.max(st, axis=0, keepdims=True)

    def accumulate(jj, s_ref, c_ref):
        vtj = vt_ref[jj]
        for c in range(2):
            m_old = m_ref[c]
            m_new = jnp.maximum(m_old, c_ref[c])
            alpha = jnp.exp2(m_old - m_new)
            pt = jnp.exp2(s_ref[c] - m_new)
            l_ref[c] = alpha * l_ref[c] + jnp.sum(pt, axis=0, keepdims=True)
            acc_ref[c] = alpha * acc_ref[c] + jnp.dot(vtj, pt.astype(BF16),
                                                      preferred_element_type=F32)
            m_ref[c] = m_new

    scores(0, sa_ref, ca_ref)

    def pair(t, carry):
        scores(2 * t + 1, sb_ref, cb_ref)
        accumulate(2 * t, sa_ref, ca_ref)
        scores(2 * t + 2, sa_ref, ca_ref)
        accumulate(2 * t + 1, sb_ref, cb_ref)
        return carry

    lax.fori_loop(0, (n_kv - 1) // 2, pair, 0)
    if n_kv % 2 == 0:
        scores(n_kv - 1, sb_ref, cb_ref)
        accumulate(n_kv - 2, sa_ref, ca_ref)
        accumulate(n_kv - 1, sb_ref, cb_ref)
    else:
        accumulate(n_kv - 1, sa_ref, ca_ref)

    lp = lam_ref[...]
    lam = (jnp.exp(jnp.sum(lp[0:1] * lp[1:2], axis=-1, keepdims=True))
           - jnp.exp(jnp.sum(lp[2:3] * lp[3:4], axis=-1, keepdims=True)) + lam_init)
    ot = acc_ref[0] / l_ref[0] - lam * (acc_ref[1] / l_ref[1])
    ms = jnp.mean(ot * ot, axis=0, keepdims=True)
    ont = ot * lax.rsqrt(ms + NORM_EPS) * (nw_ref[...] * (1.0 - lam_init))
    o_ref[...] = ont.T.astype(BF16)


def _diff_attn(qk, vt, lam_p, norm_w, *, n_lat, lam_init):
    rows = qk.shape[0]
    d = DIFF_HEADS * DIFF_DV
    n_kv, tk = vt.shape[1], vt.shape[3]
    tq = 512
    return pl.pallas_call(
        functools.partial(_attn_kernel, tk=tk, n_kv=n_kv, lam_init=lam_init),
        grid=(DIFF_HEADS, n_lat // tq),
        in_specs=[pl.BlockSpec((4, DIFF_DH), lambda h, i: (0, 0)),
                  pl.BlockSpec((tq, DIFF_DV), lambda h, i: (i, h)),
                  pl.BlockSpec((rows, DIFF_DV), lambda h, i: (0, DIFF_HEADS + h)),
                  pl.BlockSpec((None, n_kv, DIFF_DV, tk), lambda h, i: (h, 0, 0, 0)),
                  pl.BlockSpec((DIFF_DV, 1), lambda h, i: (0, 0))],
        out_specs=pl.BlockSpec((tq, DIFF_DV), lambda h, i: (i, h)),
        out_shape=jax.ShapeDtypeStruct((n_lat, d), BF16),
        scratch_shapes=[pltpu.VMEM((2, 1, tq), F32), pltpu.VMEM((2, 1, tq), F32),
                        pltpu.VMEM((2, DIFF_DV, tq), F32),
                        pltpu.VMEM((2, tk, tq), F32), pltpu.VMEM((2, tk, tq), F32),
                        pltpu.VMEM((2, 1, tq), F32), pltpu.VMEM((2, 1, tq), F32)],
        compiler_params=_cparams(("arbitrary", "arbitrary"), 48),
        name="diff_attn",
    )(lam_p, qk, qk, vt, norm_w.reshape(DIFF_DV, 1))


def _outproj_kernel(h_ref, mod_ref, x_ref, w_ref, o_ref):
    y = jnp.dot(x_ref[...], w_ref[...], preferred_element_type=F32)
    o_ref[...] = h_ref[...] + mod_ref[0, 2:3, :] * y


def _outproj_latent(h, mod3, x, w, *, n_lat):
    d = h.shape[1]
    tm = 512
    return pl.pallas_call(
        _outproj_kernel,
        grid=(n_lat // tm,),
        in_specs=[pl.BlockSpec((tm, d), lambda i: (i, 0)),
                  pl.BlockSpec((2, 3, d), lambda i: (0, 0, 0)),
                  pl.BlockSpec((tm, d), lambda i: (i, 0)),
                  pl.BlockSpec((d, d), lambda i: (0, 0))],
        out_specs=pl.BlockSpec((tm, d), lambda i: (i, 0)),
        out_shape=jax.ShapeDtypeStruct((n_lat, d), F32),
        compiler_params=_cparams(("arbitrary",), 40),
        name="attn_outproj",
    )(h, mod3, x, w)


def _rope_tables(n_lat, n_ctx):
    rows = n_lat // GRID_W
    row = jnp.repeat(jnp.arange(rows), GRID_W).astype(F32)
    col = jnp.tile(jnp.arange(GRID_W), rows).astype(F32)
    n_freq = DIFF_DH // 4
    inv_freq = ROPE_THETA ** (-jnp.arange(n_freq, dtype=F32) / n_freq)
    ang = jnp.concatenate([row[:, None] * inv_freq, col[:, None] * inv_freq], axis=-1)
    cos, sin = jnp.cos(ang), jnp.sin(ang)
    cos_t = jnp.concatenate([cos, cos], axis=-1)
    sin_t = jnp.concatenate([-sin, sin], axis=-1)
    cos_t = jnp.concatenate([cos_t, jnp.ones((n_ctx, DIFF_DH), F32)], axis=0)
    sin_t = jnp.concatenate([sin_t, jnp.zeros((n_ctx, DIFF_DH), F32)], axis=0)
    return cos_t, sin_t


def kernel(x, c, ctx, c_ctx, ada_w, ada_b, ffn_w1, ffn_w3, ffn_w2, gla_w_in, gla_gate_w2,
           gla_gate_b, gla_norm_w, pool_w, pool_scale, mix0_w_out, diff_w_qkv, diff_lambda,
           diff_norm_w, diff_w_out, final_norm_w):
    assert x.shape[0] == 1 and ada_w.shape[0] == 2
    n_lat, d = x.shape[1], x.shape[2]
    n_ctx = ctx.shape[1]
    rows = n_lat + n_ctx
    tm = 768
    assert rows % tm == 0 and n_lat % 512 == 0 and n_ctx % 256 == 0 and n_lat % GRID_W == 0

    h = jnp.concatenate([x[0], ctx[0]], axis=0)
    c8 = jnp.zeros((8, d), F32).at[0].set(c[0]).at[1].set(c_ctx)
    mods = _ada(c8, ada_w, ada_b)[:, :2].reshape(2, 2, 9, d)

    w1 = ffn_w1.astype(BF16)
    w3 = ffn_w3.astype(BF16)
    w2 = ffn_w2.astype(BF16)

    m = mods[0]
    h = _ffn(h, m[:, 0:3], w1, w3, w2, 0, 0, rows=rows, tm=tm, n_lat=n_lat)
    w_in = gla_w_in[0]
    n_qkvr = 2 * GLA_HEADS * GLA_DK + 2 * GLA_HEADS * GLA_DV
    n_gate = 2 * GLA_GATE_RANK
    w_main = jnp.concatenate([w_in[:, :n_qkvr], w_in[:, n_qkvr + n_gate:]], axis=1).astype(BF16)
    w_gate = jnp.pad(w_in[:, n_qkvr:n_qkvr + n_gate], ((0, 0), (0, LANES - n_gate))).astype(BF16)
    gw2 = gla_gate_w2[0].astype(BF16)
    gw2p = jnp.zeros((2, LANES, gw2.shape[-1]), BF16)
    gw2p = gw2p.at[0, :GLA_GATE_RANK].set(gw2[0]).at[1, GLA_GATE_RANK:n_gate].set(gw2[1])
    p0, lf, lb = _proj0(h, m[:, 3:6], w_main, w_gate, gw2p, gla_gate_b[0], tm=tm, n_lat=n_lat)
    v_lo = 2 * GLA_HEADS * GLA_DK
    v16 = p0[:, v_lo:v_lo + GLA_HEADS * GLA_DV].astype(BF16)
    o_f, o_b = _gla(p0, v16, lf, lb, n_lat=n_lat)
    h = _mix0_out(h, m[:, 3:6], o_f, o_b, p0, gla_norm_w[0], pool_w[0].astype(BF16),
                  pool_scale[0], mix0_w_out[0].astype(BF16), n_lat=n_lat, n_ctx=n_ctx)
    h = _ffn(h, m[:, 6:9], w1, w3, w2, 0, 1, rows=rows, tm=tm, n_lat=n_lat)

    m = mods[1]
    h = _ffn(h, m[:, 0:3], w1, w3, w2, 1, 0, rows=rows, tm=tm, n_lat=n_lat)
    cos_t, sin_t = _rope_tables(n_lat, n_ctx)
    w_qk = diff_w_qkv[0, :, :2 * d].astype(BF16)
    w_vt = diff_w_qkv[0, :, 2 * d:].T.astype(BF16)
    qk, vt = _proj1(h, m[:, 3:6], w_qk, w_vt, cos_t, sin_t, tm=tm, n_lat=n_lat)
    lam_init = 0.8 - 0.6 * math.exp(-0.3 * 1)
    o = _diff_attn(qk, vt, diff_lambda[0], diff_norm_w[0], n_lat=n_lat, lam_init=lam_init)
    hl = _outproj_latent(h, m[:, 3:6], o, diff_w_out[0].astype(BF16), n_lat=n_lat)
    out = _ffn(hl, m[:, 6:9], w1, w3, w2, 1, 1, rows=n_lat, tm=512, n_lat=n_lat,
               final_w=final_norm_w)
    return out[None]
```

```python
import functools
import math

import jax
import jax.numpy as jnp
from jax import lax
from jax.experimental import pallas as pl
from jax.experimental.pallas import tpu as pltpu

F32 = jnp.float32
BF16 = jnp.bfloat16

NORM_EPS = 1e-6
GRID_W = 64
ROPE_THETA = 10000.0

GLA_HEADS = 4
GLA_DK = 128
GLA_DV = 256
GLA_GATE_RANK = 16
GLA_TAU = 16.0
GLA_CHUNK = 64
GLA_SUB = 16
GLA_GROUP = 256
POOL_WINDOWS = (2, 4, 8, 16)
POOL_GC = 256
POOL_HALO = 8

DIFF_HEADS = 8
DIFF_DH = 128
DIFF_DV = 256

LANES = 128
MIB = 1024 * 1024


def _cparams(semantics, vmem_mib):
    return pltpu.CompilerParams(dimension_semantics=semantics,
                                vmem_limit_bytes=int(vmem_mib * MIB))


def _rms(x):
    return x * lax.rsqrt(jnp.mean(x * x, axis=-1, keepdims=True) + NORM_EPS)


def _is_ctx_rows(row0, rows, n_lat):
    return (row0 + lax.broadcasted_iota(jnp.int32, (rows, 1), 0)) >= n_lat


def _mod_row(mod_ref, is_ctx, k):
    return jnp.where(is_ctx, mod_ref[1, k:k + 1, :], mod_ref[0, k:k + 1, :])


def _modulated(h, mod_ref, is_ctx):
    return _rms(h) * (1.0 + _mod_row(mod_ref, is_ctx, 1)) + _mod_row(mod_ref, is_ctx, 0)


NORM_ROWS = 32


def _store_modulated(h_ref, mod_ref, z_ref, row0, tm, n_lat):
    def body(r, carry):
        start = pl.multiple_of(r * NORM_ROWS, NORM_ROWS)
        rows = pl.ds(start, NORM_ROWS)
        which = jnp.where(row0 + start >= n_lat, 1, 0)
        shift = mod_ref[which, 0:1, :]
        scale = mod_ref[which, 1:2, :]
        z_ref[rows, :] = (_rms(h_ref[rows, :]) * (1.0 + scale) + shift).astype(BF16)
        return carry

    lax.fori_loop(0, tm // NORM_ROWS, body, 0)


def _ada_kernel(c_ref, w_ref, b_ref, o_ref):
    c = c_ref[...]
    s = (c * jax.nn.sigmoid(c)).astype(BF16)
    o_ref[0] = jnp.dot(s, w_ref[0].astype(BF16), preferred_element_type=F32) + b_ref[0]


def _ada(c8, ada_w, ada_b):
    depth, d, n = ada_w.shape
    tn = 1024
    return pl.pallas_call(
        _ada_kernel,
        grid=(depth, n // tn),
        in_specs=[pl.BlockSpec((8, d), lambda i, j: (0, 0)),
                  pl.BlockSpec((1, d, tn), lambda i, j: (i, 0, j)),
                  pl.BlockSpec((1, 1, tn), lambda i, j: (i, 0, j))],
        out_specs=pl.BlockSpec((1, 8, tn), lambda i, j: (i, 0, j)),
        out_shape=jax.ShapeDtypeStruct((depth, 8, n), F32),
        compiler_params=_cparams(("arbitrary", "arbitrary"), 40),
        name="ada_mod",
    )(c8, ada_w, ada_b.reshape(depth, 1, n))


def _ffn_kernel(*refs, n_lat, tm, final):
    if final:
        h_ref, mod_ref, w1_ref, w3_ref, w2_ref, fnw_ref, o_ref, z_ref = refs
    else:
        h_ref, mod_ref, w1_ref, w3_ref, w2_ref, o_ref, z_ref = refs
    i = pl.program_id(0)
    j = pl.program_id(1)
    is_ctx = _is_ctx_rows(i * tm, tm, n_lat)

    @pl.when(j == 0)
    def _():
        _store_modulated(h_ref, mod_ref, z_ref, i * tm, tm, n_lat)
        o_ref[...] = jnp.zeros_like(o_ref)

    z = z_ref[...]
    u = jnp.dot(z, w1_ref[...], preferred_element_type=F32)
    g = jnp.dot(z, w3_ref[...], preferred_element_type=F32)
    a = (u * jax.nn.sigmoid(u) * g).astype(BF16)
    o_ref[...] += jnp.dot(a, w2_ref[...], preferred_element_type=F32)

    @pl.when(j == pl.num_programs(1) - 1)
    def _():
        hn = h_ref[...] + 0.5 * _mod_row(mod_ref, is_ctx, 2) * o_ref[...]
        if final:
            hn = _rms(hn) * fnw_ref[...]
        o_ref[...] = hn


def _ffn(h, mod3, w1, w3, w2, layer, half, *, rows, tm, n_lat, final_w=None):
    d = h.shape[1]
    dff = w1.shape[-1]
    tf = 512
    final = final_w is not None
    in_specs = [pl.BlockSpec((tm, d), lambda i, j: (i, 0)),
                pl.BlockSpec((2, 3, d), lambda i, j: (0, 0, 0)),
                pl.BlockSpec((None, None, d, tf), lambda i, j: (layer, half, 0, j)),
                pl.BlockSpec((None, None, d, tf), lambda i, j: (layer, half, 0, j)),
                pl.BlockSpec((None, None, tf, d), lambda i, j: (layer, half, j, 0))]
    args = [h, mod3, w1, w3, w2]
    if final:
        in_specs.append(pl.BlockSpec((1, d), lambda i, j: (0, 0)))
        args.append(final_w.reshape(1, d))
    return pl.pallas_call(
        functools.partial(_ffn_kernel, n_lat=n_lat, tm=tm, final=final),
        grid=(rows // tm, dff // tf),
        in_specs=in_specs,
        out_specs=pl.BlockSpec((tm, d), lambda i, j: (i, 0)),
        out_shape=jax.ShapeDtypeStruct((rows, d), F32),
        scratch_shapes=[pltpu.VMEM((tm, d), BF16)],
        compiler_params=_cparams(("arbitrary", "arbitrary"), 56),
        name="ffn_final" if final else "ffn_half",
    )(*args)


def _log_sigmoid(x):
    return jnp.minimum(x, 0.0) - jnp.log1p(jnp.exp(-jnp.abs(x)))


def _proj0_kernel(h_ref, mod_ref, w_ref, wg_ref, gw2_ref, gb_ref,
                  p_ref, lf_ref, lb_ref, z_ref, *, n_lat, tm):
    i = pl.program_id(0)
    j = pl.program_id(1)

    @pl.when(j == 0)
    def _():
        _store_modulated(h_ref, mod_ref, z_ref, i * tm, tm, n_lat)
        gz = jnp.dot(z_ref[...], wg_ref[...], preferred_element_type=F32).astype(BF16)
        for d, out in ((0, lf_ref), (1, lb_ref)):
            zz = jnp.dot(gz, gw2_ref[d], preferred_element_type=F32) + gb_ref[d]
            out[...] = _log_sigmoid(zz) / GLA_TAU

    p_ref[...] = jnp.dot(z_ref[...], w_ref[...], preferred_element_type=F32)


def _proj0(h, mod3, w_main, w_gate, gate_w2p, gate_b, *, tm, n_lat):
    rows, d = h.shape
    n = w_main.shape[1]
    qk = gate_b.shape[-1]
    tn = 512
    return pl.pallas_call(
        functools.partial(_proj0_kernel, n_lat=n_lat, tm=tm),
        grid=(rows // tm, n // tn),
        in_specs=[pl.BlockSpec((tm, d), lambda i, j: (i, 0)),
                  pl.BlockSpec((2, 3, d), lambda i, j: (0, 0, 0)),
                  pl.BlockSpec((d, tn), lambda i, j: (0, j)),
                  pl.BlockSpec((d, LANES), lambda i, j: (0, 0)),
                  pl.BlockSpec((2, LANES, qk), lambda i, j: (0, 0, 0)),
                  pl.BlockSpec((2, 1, qk), lambda i, j: (0, 0, 0))],
        out_specs=[pl.BlockSpec((tm, tn), lambda i, j: (i, j)),
                   pl.BlockSpec((tm, qk), lambda i, j: (i, 0)),
                   pl.BlockSpec((tm, qk), lambda i, j: (i, 0))],
        out_shape=[jax.ShapeDtypeStruct((rows, n), F32),
                   jax.ShapeDtypeStruct((rows, qk), F32),
                   jax.ShapeDtypeStruct((rows, qk), F32)],
        scratch_shapes=[pltpu.VMEM((tm, d), BF16)],
        compiler_params=_cparams(("arbitrary", "arbitrary"), 48),
        name="proj_gla_pool",
    )(h, mod3, w_main, w_gate, gate_w2p, gate_b.reshape(2, 1, qk))


def _split3(x):
    x1 = x.astype(BF16)
    r1 = x - x1.astype(F32)
    x2 = r1.astype(BF16)
    x3 = (r1 - x2.astype(F32)).astype(BF16)
    return x1, x2, x3


def _gla_chunk(q_ref, k_ref, v_ref, g_ref, o_ref, s_ref, b_ref, base, head, reverse):
    C, SB = GLA_CHUNK, GLA_SUB
    nb = C // SB
    rows = pl.ds(base, C)
    kcols = slice(head * GLA_DK, (head + 1) * GLA_DK)
    vcols = slice(head * GLA_DV, (head + 1) * GLA_DV)
    q = q_ref[rows, kcols] * (GLA_DK ** -0.5)
    k = k_ref[rows, kcols]
    v = v_ref[rows, vcols]
    g = g_ref[rows, kcols]

    ti = lax.broadcasted_iota(jnp.int32, (C, C), 0)
    si = lax.broadcasted_iota(jnp.int32, (C, C), 1)
    tri = jnp.where((si >= ti) if reverse else (si <= ti), 1.0, 0.0).astype(BF16)
    b = sum(jnp.dot(tri, part, preferred_element_type=F32) for part in _split3(g))
    yield

    b_ref[0] = b
    b_ref[1] = k
    s_old = s_ref[...]
    inter = jnp.dot((q * jnp.exp(b)).astype(BF16), s_old.astype(BF16),
                    preferred_element_type=F32)
    last = 0 if reverse else C - 1
    b_last = b_ref[0, last:last + 1, :]
    ke = (k * jnp.exp(b_last - b)).astype(BF16)
    upd = lax.dot_general(ke, v, (((0,), (0,)), ((), ())), preferred_element_type=F32)
    raws = []
    for i in range(nb):
        blk = slice(i * SB, (i + 1) * SB)
        edge = (i + 1) * SB if reverse else i * SB - 1
        has_off = (i < nb - 1) if reverse else (i > 0)
        if has_off:
            r_i = b_ref[0, edge:edge + 1, :]
            qe = (q[blk] * jnp.exp(b[blk] - r_i)).astype(BF16)
            ke_i = (k * jnp.exp(jnp.minimum(r_i - b, 0.0))).astype(BF16)
            raws.append(lax.dot_general(qe, ke_i, (((1,), (1,)), ((), ())),
                                        preferred_element_type=F32))
        else:
            raws.append(None)
    yield

    trow = lax.broadcasted_iota(jnp.int32, (SB, LANES), 0)
    ones = jnp.ones((LANES, LANES), BF16)
    dsums = []
    for i in range(nb):
        blk = slice(i * SB, (i + 1) * SB)
        q_i = q[blk]
        b_i = b[blk]
        terms = []
        for s in range(SB):
            row = i * SB + s
            k_s = b_ref[1, row:row + 1, :]
            b_s = b_ref[0, row:row + 1, :]
            keep = (trow <= s) if reverse else (trow >= s)
            terms.append(jnp.where(keep, q_i * k_s * jnp.exp(b_i - b_s), 0.0))
        stacked = jnp.concatenate(terms, axis=0).astype(BF16)
        dsums.append(jnp.dot(stacked, ones, preferred_element_type=F32))
    yield

    col = lax.broadcasted_iota(jnp.int32, (SB, C), 1)
    lane = lax.broadcasted_iota(jnp.int32, (SB, LANES), 1)
    a_rows = []
    for i in range(nb):
        diag = jnp.zeros((SB, LANES), F32)
        for s in range(SB):
            diag = jnp.where(lane == i * SB + s, dsums[i][s * SB:(s + 1) * SB], diag)
        a_i = diag[:, :C]
        if raws[i] is not None:
            off_mask = (col >= (i + 1) * SB) if reverse else (col < i * SB)
            a_i = a_i + jnp.where(off_mask, raws[i], 0.0)
        a_rows.append(a_i)
    attn = jnp.concatenate(a_rows, axis=0).astype(BF16)
    intra = jnp.dot(attn, v, preferred_element_type=F32)
    decay = jnp.broadcast_to(jnp.exp(b_last), (GLA_DK, GLA_DK)).T
    s_ref[...] = jnp.concatenate([decay, decay], axis=1) * s_old + upd
    yield

    o_ref[rows, vcols] = inter + intra
    yield


GLA_STAGES = 5


def _gla_kernel(qf_ref, kf_ref, vf_ref, gf_ref, qb_ref, kb_ref, vb_ref, gb_ref,
                of_ref, ob_ref, *scratch):
    n_chain = 2 * GLA_HEADS
    state, stage = scratch[:n_chain], scratch[n_chain:]

    @pl.when(pl.program_id(0) == 0)
    def _():
        for s_ref in state:
            s_ref[...] = jnp.zeros_like(s_ref)

    n_sub = GLA_GROUP // GLA_CHUNK

    def body(c, carry):
        fbase = pl.multiple_of(c * GLA_CHUNK, GLA_CHUNK)
        rbase = pl.multiple_of((n_sub - 1 - c) * GLA_CHUNK, GLA_CHUNK)
        chains = []
        for head in range(GLA_HEADS):
            f, r = head, GLA_HEADS + head
            chains.append(_gla_chunk(qf_ref, kf_ref, vf_ref, gf_ref, of_ref, state[f], stage[f],
                                     fbase, head, False))
            chains.append(_gla_chunk(qb_ref, kb_ref, vb_ref, gb_ref, ob_ref, state[r], stage[r],
                                     rbase, head, True))
        for _ in range(GLA_STAGES):
            for chain in chains:
                next(chain)
        return carry

    lax.fori_loop(0, n_sub, body, 0)


def _gla(p0, v16, lf, lb, *, n_lat):
    rows = p0.shape[0]
    n_grp = rows // GLA_GROUP
    lat_grp = n_lat // GLA_GROUP
    G = GLA_GROUP
    qk_w = GLA_HEADS * GLA_DK
    v_w = GLA_HEADS * GLA_DV

    def fwd(c):
        return (c + lat_grp) % n_grp

    def bwd(c):
        return n_grp - 1 - c

    def spec(width, order, col):
        return pl.BlockSpec((G, width), lambda c: (order(c), col))

    in_specs = []
    for order in (fwd, bwd):
        in_specs += [spec(qk_w, order, 0), spec(qk_w, order, 1), spec(v_w, order, 0),
                     spec(qk_w, order, 0)]
    out_specs = [spec(v_w, fwd, 0), spec(v_w, bwd, 0)]
    return pl.pallas_call(
        _gla_kernel,
        grid=(n_grp,),
        in_specs=in_specs,
        out_specs=out_specs,
        out_shape=[jax.ShapeDtypeStruct((rows, v_w), F32)] * 2,
        scratch_shapes=([pltpu.VMEM((GLA_DK, GLA_DV), F32)] * (2 * GLA_HEADS)
                        + [pltpu.VMEM((2, GLA_CHUNK, GLA_DK), F32)] * (2 * GLA_HEADS)),
        compiler_params=_cparams(("arbitrary",), 32),
        name="gla_scan",
    )(p0, p0, v16, lf, p0, p0, v16, lb)


def _mix0_out_kernel(h_ref, mod_ref, of_ref, ob_ref, r_ref, uprev_ref, u_ref, unext_ref,
                     nw_ref, pw_ref, ps_ref, wout_ref, o_ref, ext_ref,
                     *, n_lat, n_ctx, tm):
    i = pl.program_id(0)
    lat_tiles = n_lat // tm
    n_tiles = (n_lat + n_ctx) // tm
    in_ctx = i >= lat_tiles
    is_first = (i == 0) | (i == lat_tiles)
    is_last = (i == lat_tiles - 1) | (i == n_tiles - 1)

    o = of_ref[...] + ob_ref[...]
    r = r_ref[...]
    heads = []
    for hd in range(GLA_HEADS):
        heads.append(_rms(o[:, hd * GLA_DV:(hd + 1) * GLA_DV]) * nw_ref[...])
    gl = jnp.concatenate(heads, axis=1) * (r * jax.nn.sigmoid(r))

    u = u_ref[...]
    H = POOL_HALO
    ext_ref[0:H, :] = jnp.where(is_first, 0.0, uprev_ref[...])
    ext_ref[H:H + tm, :] = u
    ext_ref[H + tm:2 * H + tm, :] = jnp.where(is_last, 0.0, unext_ref[...])
    seq_len = jnp.where(in_ctx, n_ctx, n_lat)
    t = (i - jnp.where(in_ctx, lat_tiles, 0)) * tm + lax.broadcasted_iota(jnp.int32, (tm, 1), 0)
    pooled = []
    for gi, w in enumerate(POOL_WINDOWS):
        cols = slice(gi * POOL_GC, (gi + 1) * POOL_GC)
        s = ext_ref[H - w // 2:H - w // 2 + tm, cols]
        for dlt in range(-w // 2 + 1, w - w // 2):
            s = s + ext_ref[H + dlt:H + dlt + tm, cols]
        lo = jnp.maximum(t - w // 2, 0)
        hi = jnp.minimum(t + (w - w // 2), seq_len)
        cnt = (hi - lo).astype(F32)
        pg = (s / cnt - u[:, cols]).astype(BF16)
        pooled.append(jnp.dot(pg, pw_ref[gi], preferred_element_type=F32) * ps_ref[:, cols])

    mix = jnp.concatenate([gl] + pooled, axis=1).astype(BF16)
    y = jnp.dot(mix, wout_ref[...], preferred_element_type=F32)
    is_ctx = _is_ctx_rows(i * tm, tm, n_lat)
    o_ref[...] = h_ref[...] + _mod_row(mod_ref, is_ctx, 2) * y


def _mix0_out(h, mod3, o_f, o_b, p0, norm_w, pool_w, pool_scale, w_out, *, n_lat, n_ctx):
    rows, d = h.shape
    tm = 256
    gw = GLA_HEADS * GLA_DV
    pwid = len(POOL_WINDOWS) * POOL_GC
    hb = tm // POOL_HALO
    n_hblk = rows // POOL_HALO
    r_col = (2 * GLA_HEADS * GLA_DK + gw) // gw
    u_col = (2 * GLA_HEADS * GLA_DK + 2 * gw) // pwid
    return pl.pallas_call(
        functools.partial(_mix0_out_kernel, n_lat=n_lat, n_ctx=n_ctx, tm=tm),
        grid=(rows // tm,),
        in_specs=[pl.BlockSpec((tm, d), lambda i: (i, 0)),
                  pl.BlockSpec((2, 3, d), lambda i: (0, 0, 0)),
                  pl.BlockSpec((tm, gw), lambda i: (i, 0)),
                  pl.BlockSpec((tm, gw), lambda i: (i, 0)),
                  pl.BlockSpec((tm, gw), lambda i: (i, r_col)),
                  pl.BlockSpec((POOL_HALO, pwid), lambda i: (jnp.maximum(i * hb - 1, 0), u_col)),
                  pl.BlockSpec((tm, pwid), lambda i: (i, u_col)),
                  pl.BlockSpec((POOL_HALO, pwid),
                               lambda i: (jnp.minimum((i + 1) * hb, n_hblk - 1), u_col)),
                  pl.BlockSpec((1, GLA_DV), lambda i: (0, 0)),
                  pl.BlockSpec((len(POOL_WINDOWS), POOL_GC, POOL_GC), lambda i: (0, 0, 0)),
                  pl.BlockSpec((1, pwid), lambda i: (0, 0)),
                  pl.BlockSpec((d, d), lambda i: (0, 0))],
        out_specs=pl.BlockSpec((tm, d), lambda i: (i, 0)),
        out_shape=jax.ShapeDtypeStruct((rows, d), F32),
        scratch_shapes=[pltpu.VMEM((tm + 2 * POOL_HALO, pwid), F32)],
        compiler_params=_cparams(("arbitrary",), 48),
        name="mix0_readout",
    )(h, mod3, o_f, o_b, p0, p0, p0, p0, norm_w.reshape(1, GLA_DV), pool_w,
      pool_scale.reshape(1, pwid), w_out)


def _proj1_kernel(h_ref, mod_ref, wqk_ref, wvt_ref, cos_ref, sin_ref, qk_ref, vt_ref, z_ref,
                  *, n_lat, tm, tn, n_qt, qscale):
    i = pl.program_id(0)
    j = pl.program_id(1)

    @pl.when(j == 0)
    def _():
        _store_modulated(h_ref, mod_ref, z_ref, i * tm, tm, n_lat)

    @pl.when(j < 2 * n_qt)
    def _():
        acc = jnp.dot(z_ref[...], wqk_ref[...], preferred_element_type=F32)
        scale = jnp.where(j < n_qt, qscale, 1.0)
        cs = cos_ref[...] * scale
        sn = sin_ref[...] * scale
        for gidx in range(tn // LANES):
            x = acc[:, gidx * LANES:(gidx + 1) * LANES]
            y = x * cs + pltpu.roll(x, LANES // 2, 1) * sn
            qk_ref[:, gidx * LANES:(gidx + 1) * LANES] = y.astype(BF16)

    @pl.when(j >= 2 * n_qt)
    def _():
        vt = lax.dot_general(wvt_ref[...], z_ref[...], (((1,), (1,)), ((), ())),
                             preferred_element_type=F32)
        for hd in range(tn // DIFF_DV):
            vt_ref[hd] = vt[hd * DIFF_DV:(hd + 1) * DIFF_DV].astype(BF16)


def _proj1(h, mod3, w_qk, w_vt, cos_t, sin_t, *, tm, n_lat):
    rows, d = h.shape
    tn = 512
    n_qt = d // tn
    n_vt = w_vt.shape[0] // tn
    hpt = tn // DIFF_DV
    qscale = (DIFF_DH ** -0.5) * math.log2(math.e)
    return pl.pallas_call(
        functools.partial(_proj1_kernel, n_lat=n_lat, tm=tm, tn=tn, n_qt=n_qt, qscale=qscale),
        grid=(rows // tm, 2 * n_qt + n_vt),
        in_specs=[pl.BlockSpec((tm, d), lambda i, j: (i, 0)),
                  pl.BlockSpec((2, 3, d), lambda i, j: (0, 0, 0)),
                  pl.BlockSpec((d, tn), lambda i, j: (0, jnp.minimum(j, 2 * n_qt - 1))),
                  pl.BlockSpec((tn, d), lambda i, j: (jnp.maximum(j - 2 * n_qt, 0), 0)),
                  pl.BlockSpec((tm, LANES), lambda i, j: (i, 0)),
                  pl.BlockSpec((tm, LANES), lambda i, j: (i, 0))],
        out_specs=[pl.BlockSpec((tm, tn), lambda i, j: (i, jnp.minimum(j, 2 * n_qt - 1))),
                   pl.BlockSpec((hpt, None, DIFF_DV, tm),
                                lambda i, j: (jnp.maximum(j - 2 * n_qt, 0), i, 0, 0))],
        out_shape=[jax.ShapeDtypeStruct((rows, 2 * d), BF16),
                   jax.ShapeDtypeStruct((DIFF_HEADS, rows // tm, DIFF_DV, tm), BF16)],
        scratch_shapes=[pltpu.VMEM((tm, d), BF16)],
        compiler_params=_cparams(("arbitrary", "arbitrary"), 40),
        name="proj_qkv_rope",
    )(h, mod3, w_qk, w_vt, cos_t, sin_t)


ATT_TILE = 256


def _attn_kernel(lam_ref, q_ref, k_ref, vt_ref, nw_ref, o_ref, m_ref, l_ref, acc_ref,
                 sa_ref, sb_ref, ca_ref, cb_ref, p_ref, *, tk, n_kv, lam_init):
    tq = q_ref.shape[0]
    T = ATT_TILE
    SUB = 8
    m_ref[...] = jnp.full_like(m_ref, -jnp.inf)
    l_ref[...] = jnp.zeros_like(l_ref)
    acc_ref[...] = jnp.zeros_like(acc_ref)

    def scores(jj, s_ref, c_ref):
        for c in range(2):
            comp = slice(c * DIFF_DH, (c + 1) * DIFF_DH)
            for qb in range(tq // T):
                cols = slice(qb * T, (qb + 1) * T)
                qc = q_ref[cols, comp]
                cmax = None
                for kb in range(tk // T):
                    kc = k_ref[jj * tk + kb * T:jj * tk + (kb + 1) * T, comp]
                    st = lax.dot_general(kc, qc, (((1,), (1,)), ((), ())),
                                         preferred_element_type=F32)
                    s_ref[c, kb * T:(kb + 1) * T, cols] = st
                    bmax = jnp.max(st.reshape(T // SUB, SUB, T), axis=0)
                    cmax = bmax if cmax is None else jnp.maximum(cmax, bmax)
                c_ref[c, :, cols] = cmax

    def accumulate(jj, s_ref, c_ref):
        for c in range(2):
            for qb in range(tq // T):
                cols = slice(qb * T, (qb + 1) * T)
                m_old = m_ref[c, :, cols]
                m_new = jnp.maximum(m_old, jnp.max(c_ref[c, :, cols], axis=0, keepdims=True))
                alpha = jnp.exp2(m_old - m_new)
                lsum = jnp.zeros((SUB, T), F32)
                for kb in range(tk // T):
                    keys = slice(kb * T, (kb + 1) * T)
                    pt = jnp.exp2(s_ref[c, keys, cols] - m_new)
                    lsum = lsum + jnp.sum(pt.reshape(T // SUB, SUB, T), axis=0)
                    p_ref[c, keys, cols] = pt.astype(BF16)
                pv = jnp.dot(vt_ref[jj], p_ref[c, :, cols], preferred_element_type=F32)
                l_ref[c, :, cols] = alpha * l_ref[c, :, cols] + lsum
                acc_ref[c, :, cols] = alpha * acc_ref[c, :, cols] + pv
                m_ref[c, :, cols] = m_new

    slots = ((sa_ref, ca_ref), (sb_ref, cb_ref))
    scores(0, *slots[0])
    for jj in range(n_kv):
        if jj + 1 < n_kv:
            scores(jj + 1, *slots[(jj + 1) % 2])
        accumulate(jj, *slots[jj % 2])

    lp = lam_ref[...]
    lam = (jnp.exp(jnp.sum(lp[0:1] * lp[1:2], axis=-1, keepdims=True))
           - jnp.exp(jnp.sum(lp[2:3] * lp[3:4], axis=-1, keepdims=True)) + lam_init)
    l0 = jnp.sum(l_ref[0], axis=0, keepdims=True)
    l1 = jnp.sum(l_ref[1], axis=0, keepdims=True)
    ot = acc_ref[0] / l0 - lam * (acc_ref[1] / l1)
    ms = jnp.mean(ot * ot, axis=0, keepdims=True)
    ont = ot * lax.rsqrt(ms + NORM_EPS) * (nw_ref[...] * (1.0 - lam_init))
    o_ref[...] = ont.T.astype(BF16)


def _diff_attn(qk, vt, lam_p, norm_w, *, n_lat, lam_init):
    rows = qk.shape[0]
    d = DIFF_HEADS * DIFF_DV
    n_kv, tk = vt.shape[1], vt.shape[3]
    tq = 512
    return pl.pallas_call(
        functools.partial(_attn_kernel, tk=tk, n_kv=n_kv, lam_init=lam_init),
        grid=(DIFF_HEADS, n_lat // tq),
        in_specs=[pl.BlockSpec((4, DIFF_DH), lambda h, i: (0, 0)),
                  pl.BlockSpec((tq, DIFF_DV), lambda h, i: (i, h)),
                  pl.BlockSpec((rows, DIFF_DV), lambda h, i: (0, DIFF_HEADS + h)),
                  pl.BlockSpec((None, n_kv, DIFF_DV, tk), lambda h, i: (h, 0, 0, 0)),
                  pl.BlockSpec((DIFF_DV, 1), lambda h, i: (0, 0))],
        out_specs=pl.BlockSpec((tq, DIFF_DV), lambda h, i: (i, h)),
        out_shape=jax.ShapeDtypeStruct((n_lat, d), BF16),
        scratch_shapes=[pltpu.VMEM((2, 1, tq), F32), pltpu.VMEM((2, 8, tq), F32),
                        pltpu.VMEM((2, DIFF_DV, tq), F32),
                        pltpu.VMEM((2, tk, tq), F32), pltpu.VMEM((2, tk, tq), F32),
                        pltpu.VMEM((2, 8, tq), F32), pltpu.VMEM((2, 8, tq), F32),
                        pltpu.VMEM((2, tk, tq), BF16)],
        compiler_params=_cparams(("arbitrary", "arbitrary"), 48),
        name="diff_attn",
    )(lam_p, qk, qk, vt, norm_w.reshape(DIFF_DV, 1))


def _outproj_kernel(h_ref, mod_ref, x_ref, w_ref, o_ref):
    y = jnp.dot(x_ref[...], w_ref[...], preferred_element_type=F32)
    o_ref[...] = h_ref[...] + mod_ref[0, 2:3, :] * y


def _outproj_latent(h, mod3, x, w, *, n_lat):
    d = h.shape[1]
    tm = 512
    return pl.pallas_call(
        _outproj_kernel,
        grid=(n_lat // tm,),
        in_specs=[pl.BlockSpec((tm, d), lambda i: (i, 0)),
                  pl.BlockSpec((2, 3, d), lambda i: (0, 0, 0)),
                  pl.BlockSpec((tm, d), lambda i: (i, 0)),
                  pl.BlockSpec((d, d), lambda i: (0, 0))],
        out_specs=pl.BlockSpec((tm, d), lambda i: (i, 0)),
        out_shape=jax.ShapeDtypeStruct((n_lat, d), F32),
        compiler_params=_cparams(("arbitrary",), 40),
        name="attn_outproj",
    )(h, mod3, x, w)


def _rope_tables(n_lat, n_ctx):
    rows = n_lat // GRID_W
    row = jnp.repeat(jnp.arange(rows), GRID_W).astype(F32)
    col = jnp.tile(jnp.arange(GRID_W), rows).astype(F32)
    n_freq = DIFF_DH // 4
    inv_freq = ROPE_THETA ** (-jnp.arange(n_freq, dtype=F32) / n_freq)
    ang = jnp.concatenate([row[:, None] * inv_freq, col[:, None] * inv_freq], axis=-1)
    cos, sin = jnp.cos(ang), jnp.sin(ang)
    cos_t = jnp.concatenate([cos, cos], axis=-1)
    sin_t = jnp.concatenate([-sin, sin], axis=-1)
    cos_t = jnp.concatenate([cos_t, jnp.ones((n_ctx, DIFF_DH), F32)], axis=0)
    sin_t = jnp.concatenate([sin_t, jnp.zeros((n_ctx, DIFF_DH), F32)], axis=0)
    return cos_t, sin_t


def kernel(x, c, ctx, c_ctx, ada_w, ada_b, ffn_w1, ffn_w3, ffn_w2, gla_w_in, gla_gate_w2,
           gla_gate_b, gla_norm_w, pool_w, pool_scale, mix0_w_out, diff_w_qkv, diff_lambda,
           diff_norm_w, diff_w_out, final_norm_w):
    assert x.shape[0] == 1 and ada_w.shape[0] == 2
    n_lat, d = x.shape[1], x.shape[2]
    n_ctx = ctx.shape[1]
    rows = n_lat + n_ctx
    tm = 768
    assert rows % tm == 0 and n_lat % 512 == 0 and n_ctx % 256 == 0 and n_lat % GRID_W == 0

    h = jnp.concatenate([x[0], ctx[0]], axis=0)
    c8 = jnp.zeros((8, d), F32).at[0].set(c[0]).at[1].set(c_ctx)
    mods = _ada(c8, ada_w, ada_b)[:, :2].reshape(2, 2, 9, d)

    w1 = ffn_w1.astype(BF16)
    w3 = ffn_w3.astype(BF16)
    w2 = ffn_w2.astype(BF16)

    m = mods[0]
    h = _ffn(h, m[:, 0:3], w1, w3, w2, 0, 0, rows=rows, tm=tm, n_lat=n_lat)
    w_in = gla_w_in[0]
    n_qkvr = 2 * GLA_HEADS * GLA_DK + 2 * GLA_HEADS * GLA_DV
    n_gate = 2 * GLA_GATE_RANK
    w_main = jnp.concatenate([w_in[:, :n_qkvr], w_in[:, n_qkvr + n_gate:]], axis=1).astype(BF16)
    w_gate = jnp.pad(w_in[:, n_qkvr:n_qkvr + n_gate], ((0, 0), (0, LANES - n_gate))).astype(BF16)
    gw2 = gla_gate_w2[0].astype(BF16)
    gw2p = jnp.zeros((2, LANES, gw2.shape[-1]), BF16)
    gw2p = gw2p.at[0, :GLA_GATE_RANK].set(gw2[0]).at[1, GLA_GATE_RANK:n_gate].set(gw2[1])
    p0, lf, lb = _proj0(h, m[:, 3:6], w_main, w_gate, gw2p, gla_gate_b[0], tm=tm, n_lat=n_lat)
    v_lo = 2 * GLA_HEADS * GLA_DK
    v16 = p0[:, v_lo:v_lo + GLA_HEADS * GLA_DV].astype(BF16)
    o_f, o_b = _gla(p0, v16, lf, lb, n_lat=n_lat)
    h = _mix0_out(h, m[:, 3:6], o_f, o_b, p0, gla_norm_w[0], pool_w[0].astype(BF16),
                  pool_scale[0], mix0_w_out[0].astype(BF16), n_lat=n_lat, n_ctx=n_ctx)
    h = _ffn(h, m[:, 6:9], w1, w3, w2, 0, 1, rows=rows, tm=tm, n_lat=n_lat)

    m = mods[1]
    h = _ffn(h, m[:, 0:3], w1, w3, w2, 1, 0, rows=rows, tm=tm, n_lat=n_lat)
    cos_t, sin_t = _rope_tables(n_lat, n_ctx)
    w_qk = diff_w_qkv[0, :, :2 * d].astype(BF16)
    w_vt = diff_w_qkv[0, :, 2 * d:].T.astype(BF16)
    qk, vt = _proj1(h, m[:, 3:6], w_qk, w_vt, cos_t, sin_t, tm=tm, n_lat=n_lat)
    lam_init = 0.8 - 0.6 * math.exp(-0.3 * 1)
    o = _diff_attn(qk, vt, diff_lambda[0], diff_norm_w[0], n_lat=n_lat, lam_init=lam_init)
    hl = _outproj_latent(h, m[:, 3:6], o, diff_w_out[0].astype(BF16), n_lat=n_lat)
    out = _ffn(hl, m[:, 6:9], w1, w3, w2, 1, 1, rows=n_lat, tm=512, n_lat=n_lat,
               final_w=final_norm_w)
    return out[None]
```

```python
import functools
import math

import jax
import jax.numpy as jnp
from jax import lax
from jax.experimental import pallas as pl
from jax.experimental.pallas import tpu as pltpu

F32 = jnp.float32
BF16 = jnp.bfloat16

NORM_EPS = 1e-6
GRID_W = 64
ROPE_THETA = 10000.0

GLA_HEADS = 4
GLA_DK = 128
GLA_DV = 256
GLA_GATE_RANK = 16
GLA_TAU = 16.0
GLA_CHUNK = 64
GLA_SUB = 16
GLA_GROUP = 256
POOL_WINDOWS = (2, 4, 8, 16)
POOL_GC = 256
POOL_HALO = 8

DIFF_HEADS = 8
DIFF_DH = 128
DIFF_DV = 256

LANES = 128
MIB = 1024 * 1024


def _cparams(semantics, vmem_mib):
    return pltpu.CompilerParams(dimension_semantics=semantics,
                                vmem_limit_bytes=int(vmem_mib * MIB))


def _rms(x):
    return x * lax.rsqrt(jnp.mean(x * x, axis=-1, keepdims=True) + NORM_EPS)


def _is_ctx_rows(row0, rows, n_lat):
    return (row0 + lax.broadcasted_iota(jnp.int32, (rows, 1), 0)) >= n_lat


def _mod_row(mod_ref, is_ctx, k):
    return jnp.where(is_ctx, mod_ref[1, k:k + 1, :], mod_ref[0, k:k + 1, :])


def _modulated(h, mod_ref, is_ctx):
    return _rms(h) * (1.0 + _mod_row(mod_ref, is_ctx, 1)) + _mod_row(mod_ref, is_ctx, 0)


NORM_ROWS = 32


def _store_modulated(h_ref, mod_ref, z_ref, row0, tm, n_lat):
    def body(r, carry):
        start = pl.multiple_of(r * NORM_ROWS, NORM_ROWS)
        rows = pl.ds(start, NORM_ROWS)
        which = jnp.where(row0 + start >= n_lat, 1, 0)
        shift = mod_ref[which, 0:1, :]
        scale = mod_ref[which, 1:2, :]
        z_ref[rows, :] = (_rms(h_ref[rows, :]) * (1.0 + scale) + shift).astype(BF16)
        return carry

    lax.fori_loop(0, tm // NORM_ROWS, body, 0, unroll=4)


def _ada_kernel(c_ref, w_ref, b_ref, o_ref):
    c = c_ref[...]
    s = (c * jax.nn.sigmoid(c)).astype(BF16)
    o_ref[0] = jnp.dot(s, w_ref[0].astype(BF16), preferred_element_type=F32) + b_ref[0]


def _ada(c8, ada_w, ada_b):
    depth, d, n = ada_w.shape
    tn = 1024
    return pl.pallas_call(
        _ada_kernel,
        grid=(depth, n // tn),
        in_specs=[pl.BlockSpec((8, d), lambda i, j: (0, 0)),
                  pl.BlockSpec((1, d, tn), lambda i, j: (i, 0, j)),
                  pl.BlockSpec((1, 1, tn), lambda i, j: (i, 0, j))],
        out_specs=pl.BlockSpec((1, 8, tn), lambda i, j: (i, 0, j)),
        out_shape=jax.ShapeDtypeStruct((depth, 8, n), F32),
        compiler_params=_cparams(("arbitrary", "arbitrary"), 40),
        name="ada_mod",
    )(c8, ada_w, ada_b.reshape(depth, 1, n))


def _ffn_kernel(*refs, n_lat, tm, final, starts):
    if final:
        h_ref, mod_ref, w1_ref, w3_ref, w2_ref, fnw_ref, o_ref, z_ref = refs
        casts = ()
    else:
        (h_ref, mod_ref, w1_ref, w3_ref, w2_ref, n1_ref, n3_ref, n2_ref,
         o_ref, c1_ref, c3_ref, c2_ref, z_ref) = refs
        casts = ((n1_ref, c1_ref), (n3_ref, c3_ref), (n2_ref, c2_ref))
    i = pl.program_id(0)
    j = pl.program_id(1)
    is_ctx = _is_ctx_rows(i * tm, tm, n_lat)

    @pl.when(j == 0)
    def _():
        _store_modulated(h_ref, mod_ref, z_ref, i * tm, tm, n_lat)
        o_ref[...] = jnp.zeros_like(o_ref)

    step = i * pl.num_programs(1) + j
    for m, (src_ref, dst_ref) in enumerate(casts):
        @pl.when((step >= starts[m]) & (step < starts[m + 1]))
        def _(src_ref=src_ref, dst_ref=dst_ref):
            dst_ref[...] = src_ref[...].astype(BF16)

    z = z_ref[...]
    u = jnp.dot(z, w1_ref[...], preferred_element_type=F32)
    g = jnp.dot(z, w3_ref[...], preferred_element_type=F32)
    a = (u * jax.nn.sigmoid(u) * g).astype(BF16)
    o_ref[...] += jnp.dot(a, w2_ref[...], preferred_element_type=F32)

    @pl.when(j == pl.num_programs(1) - 1)
    def _():
        hn = h_ref[...] + 0.5 * _mod_row(mod_ref, is_ctx, 2) * o_ref[...]
        if final:
            hn = _rms(hn) * fnw_ref[...]
        o_ref[...] = hn


def _ffn(h, mod3, w, *, rows, tm, n_lat, nxt=None, final_w=None):
    w1, w3, w2 = w
    d = h.shape[1]
    dff = w1.shape[-1]
    tf = 512
    nj = dff // tf
    final = final_w is not None
    in_specs = [pl.BlockSpec((tm, d), lambda i, j: (i, 0)),
                pl.BlockSpec((2, 3, d), lambda i, j: (0, 0, 0)),
                pl.BlockSpec((d, tf), lambda i, j: (0, j)),
                pl.BlockSpec((d, tf), lambda i, j: (0, j)),
                pl.BlockSpec((tf, d), lambda i, j: (j, 0))]
    args = [h, mod3, w1, w3, w2]
    out_specs = [pl.BlockSpec((tm, d), lambda i, j: (i, 0))]
    out_shape = [jax.ShapeDtypeStruct((rows, d), F32)]
    starts = ()
    if final:
        in_specs.append(pl.BlockSpec((1, d), lambda i, j: (0, 0)))
        args.append(final_w.reshape(1, d))
    else:
        f1, f3, f2, layer, half = nxt
        cc = tf
        per_row = dff // cc
        steps = (rows // tm) * nj
        cr_up, cr_down = next((a, b) for a, b in ((d // 4, d // 2), (d // 2, d // 2), (d, d))
                              if (2 * (d // a) + d // b) * per_row <= steps)
        counts = ((d // cr_up) * per_row, (d // cr_up) * per_row, (d // cr_down) * per_row)
        starts = (0, counts[0], counts[0] + counts[1], sum(counts))

        def blk(i, j, m):
            return jnp.clip(i * nj + j - starts[m], 0, counts[m] - 1)

        def up_map(m):
            return lambda i, j: (blk(i, j, m) // per_row, blk(i, j, m) % per_row)

        def down_map(m):
            return lambda i, j: (blk(i, j, m) % per_row, blk(i, j, m) // per_row)

        def stacked(index_map):
            return lambda i, j: (layer, half) + index_map(i, j)

        in_specs += [pl.BlockSpec((None, None, cr_up, cc), stacked(up_map(0))),
                     pl.BlockSpec((None, None, cr_up, cc), stacked(up_map(1))),
                     pl.BlockSpec((None, None, cc, cr_down), stacked(down_map(2)))]
        args += [f1, f3, f2]
        out_specs += [pl.BlockSpec((cr_up, cc), up_map(0)), pl.BlockSpec((cr_up, cc), up_map(1)),
                      pl.BlockSpec((cc, cr_down), down_map(2))]
        out_shape += [jax.ShapeDtypeStruct((d, dff), BF16), jax.ShapeDtypeStruct((d, dff), BF16),
                      jax.ShapeDtypeStruct((dff, d), BF16)]
    res = pl.pallas_call(
        functools.partial(_ffn_kernel, n_lat=n_lat, tm=tm, final=final, starts=starts),
        grid=(rows // tm, nj),
        in_specs=in_specs,
        out_specs=out_specs,
        out_shape=out_shape,
        scratch_shapes=[pltpu.VMEM((tm, d), BF16)],
        compiler_params=_cparams(("arbitrary", "arbitrary"), 56 if final else 60),
        name="ffn_final" if final else "ffn_half",
    )(*args)
    return res[0] if final else (res[0], tuple(res[1:]))


def _log_sigmoid(x):
    return jnp.minimum(x, 0.0) - jnp.log1p(jnp.exp(-jnp.abs(x)))


def _proj0_kernel(h_ref, mod_ref, w_ref, wg_ref, gw2_ref, gb_ref,
                  p_ref, lf_ref, lb_ref, z_ref, *, n_lat, tm):
    i = pl.program_id(0)
    j = pl.program_id(1)

    @pl.when(j == 0)
    def _():
        _store_modulated(h_ref, mod_ref, z_ref, i * tm, tm, n_lat)
        gz = jnp.dot(z_ref[...], wg_ref[...], preferred_element_type=F32).astype(BF16)
        for d, out in ((0, lf_ref), (1, lb_ref)):
            zz = jnp.dot(gz, gw2_ref[d], preferred_element_type=F32) + gb_ref[d]
            out[...] = _log_sigmoid(zz) / GLA_TAU

    p_ref[...] = jnp.dot(z_ref[...], w_ref[...], preferred_element_type=F32)


def _proj0(h, mod3, w_main, w_gate, gate_w2p, gate_b, *, tm, n_lat):
    rows, d = h.shape
    n = w_main.shape[1]
    qk = gate_b.shape[-1]
    tn = 512
    return pl.pallas_call(
        functools.partial(_proj0_kernel, n_lat=n_lat, tm=tm),
        grid=(rows // tm, n // tn),
        in_specs=[pl.BlockSpec((tm, d), lambda i, j: (i, 0)),
                  pl.BlockSpec((2, 3, d), lambda i, j: (0, 0, 0)),
                  pl.BlockSpec((d, tn), lambda i, j: (0, j)),
                  pl.BlockSpec((d, LANES), lambda i, j: (0, 0)),
                  pl.BlockSpec((2, LANES, qk), lambda i, j: (0, 0, 0)),
                  pl.BlockSpec((2, 1, qk), lambda i, j: (0, 0, 0))],
        out_specs=[pl.BlockSpec((tm, tn), lambda i, j: (i, j)),
                   pl.BlockSpec((tm, qk), lambda i, j: (i, 0)),
                   pl.BlockSpec((tm, qk), lambda i, j: (i, 0))],
        out_shape=[jax.ShapeDtypeStruct((rows, n), F32),
                   jax.ShapeDtypeStruct((rows, qk), F32),
                   jax.ShapeDtypeStruct((rows, qk), F32)],
        scratch_shapes=[pltpu.VMEM((tm, d), BF16)],
        compiler_params=_cparams(("arbitrary", "arbitrary"), 48),
        name="proj_gla_pool",
    )(h, mod3, w_main, w_gate, gate_w2p, gate_b.reshape(2, 1, qk))


def _split3(x):
    x1 = x.astype(BF16)
    r1 = x - x1.astype(F32)
    x2 = r1.astype(BF16)
    x3 = (r1 - x2.astype(F32)).astype(BF16)
    return x1, x2, x3


def _gla_chunk(q_ref, k_ref, v_ref, g_ref, o_ref, s_ref, b_ref, base, head, reverse):
    C, SB = GLA_CHUNK, GLA_SUB
    nb = C // SB
    rows = pl.ds(base, C)
    kcols = slice(head * GLA_DK, (head + 1) * GLA_DK)
    vcols = slice(head * GLA_DV, (head + 1) * GLA_DV)
    q = q_ref[rows, kcols] * (GLA_DK ** -0.5)
    k = k_ref[rows, kcols]
    v = v_ref[rows, vcols]
    g = g_ref[rows, kcols]

    ti = lax.broadcasted_iota(jnp.int32, (C, C), 0)
    si = lax.broadcasted_iota(jnp.int32, (C, C), 1)
    tri = jnp.where((si >= ti) if reverse else (si <= ti), 1.0, 0.0).astype(BF16)
    b = sum(jnp.dot(tri, part, preferred_element_type=F32) for part in _split3(g))
    yield

    b_ref[0] = b
    b_ref[1] = k
    s_old = s_ref[...]
    inter = jnp.dot((q * jnp.exp(b)).astype(BF16), s_old.astype(BF16),
                    preferred_element_type=F32)
    last = 0 if reverse else C - 1
    b_last = b_ref[0, last:last + 1, :]
    ke = (k * jnp.exp(b_last - b)).astype(BF16)
    upd = lax.dot_general(ke, v, (((0,), (0,)), ((), ())), preferred_element_type=F32)
    raws = []
    for i in range(nb):
        blk = slice(i * SB, (i + 1) * SB)
        edge = (i + 1) * SB if reverse else i * SB - 1
        has_off = (i < nb - 1) if reverse else (i > 0)
        if has_off:
            r_i = b_ref[0, edge:edge + 1, :]
            qe = (q[blk] * jnp.exp(b[blk] - r_i)).astype(BF16)
            ke_i = (k * jnp.exp(jnp.minimum(r_i - b, 0.0))).astype(BF16)
            raws.append(lax.dot_general(qe, ke_i, (((1,), (1,)), ((), ())),
                                        preferred_element_type=F32))
        else:
            raws.append(None)
    yield

    trow = lax.broadcasted_iota(jnp.int32, (SB, LANES), 0)
    ones = jnp.ones((LANES, LANES), BF16)
    dsums = []
    for i in range(nb):
        blk = slice(i * SB, (i + 1) * SB)
        q_i = q[blk]
        b_i = b[blk]
        terms = []
        for s in range(SB):
            row = i * SB + s
            k_s = b_ref[1, row:row + 1, :]
            b_s = b_ref[0, row:row + 1, :]
            keep = (trow <= s) if reverse else (trow >= s)
            terms.append(jnp.where(keep, q_i * k_s * jnp.exp(b_i - b_s), 0.0))
        stacked = jnp.concatenate(terms, axis=0).astype(BF16)
        dsums.append(jnp.dot(stacked, ones, preferred_element_type=F32))
    yield

    col = lax.broadcasted_iota(jnp.int32, (SB, C), 1)
    lane = lax.broadcasted_iota(jnp.int32, (SB, LANES), 1)
    a_rows = []
    for i in range(nb):
        diag = jnp.zeros((SB, LANES), F32)
        for s in range(SB):
            diag = jnp.where(lane == i * SB + s, dsums[i][s * SB:(s + 1) * SB], diag)
        a_i = diag[:, :C]
        if raws[i] is not None:
            off_mask = (col >= (i + 1) * SB) if reverse else (col < i * SB)
            a_i = a_i + jnp.where(off_mask, raws[i], 0.0)
        a_rows.append(a_i)
    attn = jnp.concatenate(a_rows, axis=0).astype(BF16)
    intra = jnp.dot(attn, v, preferred_element_type=F32)
    decay = jnp.broadcast_to(jnp.exp(b_last), (GLA_DK, GLA_DK)).T
    s_ref[...] = jnp.concatenate([decay, decay], axis=1) * s_old + upd
    yield

    o_ref[rows, vcols] = inter + intra
    yield


GLA_STAGES = 5


def _gla_kernel(qf_ref, kf_ref, vf_ref, gf_ref, qb_ref, kb_ref, vb_ref, gb_ref,
                of_ref, ob_ref, *scratch):
    n_chain = 2 * GLA_HEADS
    state, stage = scratch[:n_chain], scratch[n_chain:]

    @pl.when(pl.program_id(0) == 0)
    def _():
        for s_ref in state:
            s_ref[...] = jnp.zeros_like(s_ref)

    n_sub = GLA_GROUP // GLA_CHUNK

    def body(c, carry):
        fbase = pl.multiple_of(c * GLA_CHUNK, GLA_CHUNK)
        rbase = pl.multiple_of((n_sub - 1 - c) * GLA_CHUNK, GLA_CHUNK)
        chains = []
        for head in range(GLA_HEADS):
            f, r = head, GLA_HEADS + head
            chains.append(_gla_chunk(qf_ref, kf_ref, vf_ref, gf_ref, of_ref, state[f], stage[f],
                                     fbase, head, False))
            chains.append(_gla_chunk(qb_ref, kb_ref, vb_ref, gb_ref, ob_ref, state[r], stage[r],
                                     rbase, head, True))
        for _ in range(GLA_STAGES):
            for chain in chains:
                next(chain)
        return carry

    lax.fori_loop(0, n_sub, body, 0)


def _gla(p0, v16, lf, lb, *, n_lat):
    rows = p0.shape[0]
    n_grp = rows // GLA_GROUP
    lat_grp = n_lat // GLA_GROUP
    G = GLA_GROUP
    qk_w = GLA_HEADS * GLA_DK
    v_w = GLA_HEADS * GLA_DV

    def fwd(c):
        return (c + lat_grp) % n_grp

    def bwd(c):
        return n_grp - 1 - c

    def spec(width, order, col):
        return pl.BlockSpec((G, width), lambda c: (order(c), col))

    in_specs = []
    for order in (fwd, bwd):
        in_specs += [spec(qk_w, order, 0), spec(qk_w, order, 1), spec(v_w, order, 0),
                     spec(qk_w, order, 0)]
    out_specs = [spec(v_w, fwd, 0), spec(v_w, bwd, 0)]
    return pl.pallas_call(
        _gla_kernel,
        grid=(n_grp,),
        in_specs=in_specs,
        out_specs=out_specs,
        out_shape=[jax.ShapeDtypeStruct((rows, v_w), F32)] * 2,
        scratch_shapes=([pltpu.VMEM((GLA_DK, GLA_DV), F32)] * (2 * GLA_HEADS)
                        + [pltpu.VMEM((2, GLA_CHUNK, GLA_DK), F32)] * (2 * GLA_HEADS)),
        compiler_params=_cparams(("arbitrary",), 32),
        name="gla_scan",
    )(p0, p0, v16, lf, p0, p0, v16, lb)


def _mix0_out_kernel(h_ref, mod_ref, of_ref, ob_ref, r_ref, uprev_ref, u_ref, unext_ref,
                     nw_ref, pw_ref, ps_ref, wout_ref, o_ref, ext_ref,
                     *, n_lat, n_ctx, tm):
    i = pl.program_id(0)
    lat_tiles = n_lat // tm
    n_tiles = (n_lat + n_ctx) // tm
    in_ctx = i >= lat_tiles
    is_first = (i == 0) | (i == lat_tiles)
    is_last = (i == lat_tiles - 1) | (i == n_tiles - 1)

    o = of_ref[...] + ob_ref[...]
    r = r_ref[...]
    heads = []
    for hd in range(GLA_HEADS):
        heads.append(_rms(o[:, hd * GLA_DV:(hd + 1) * GLA_DV]) * nw_ref[...])
    gl = jnp.concatenate(heads, axis=1) * (r * jax.nn.sigmoid(r))

    u = u_ref[...]
    H = POOL_HALO
    ext_ref[0:H, :] = jnp.where(is_first, 0.0, uprev_ref[...])
    ext_ref[H:H + tm, :] = u
    ext_ref[H + tm:2 * H + tm, :] = jnp.where(is_last, 0.0, unext_ref[...])
    seq_len = jnp.where(in_ctx, n_ctx, n_lat)
    t = (i - jnp.where(in_ctx, lat_tiles, 0)) * tm + lax.broadcasted_iota(jnp.int32, (tm, 1), 0)
    pooled = []
    for gi, w in enumerate(POOL_WINDOWS):
        cols = slice(gi * POOL_GC, (gi + 1) * POOL_GC)
        s = ext_ref[H - w // 2:H - w // 2 + tm, cols]
        for dlt in range(-w // 2 + 1, w - w // 2):
            s = s + ext_ref[H + dlt:H + dlt + tm, cols]
        lo = jnp.maximum(t - w // 2, 0)
        hi = jnp.minimum(t + (w - w // 2), seq_len)
        cnt = (hi - lo).astype(F32)
        pg = (s / cnt - u[:, cols]).astype(BF16)
        pooled.append(jnp.dot(pg, pw_ref[gi], preferred_element_type=F32) * ps_ref[:, cols])

    mix = jnp.concatenate([gl] + pooled, axis=1).astype(BF16)
    y = jnp.dot(mix, wout_ref[...], preferred_element_type=F32)
    is_ctx = _is_ctx_rows(i * tm, tm, n_lat)
    o_ref[...] = h_ref[...] + _mod_row(mod_ref, is_ctx, 2) * y


def _mix0_out(h, mod3, o_f, o_b, p0, norm_w, pool_w, pool_scale, w_out, *, n_lat, n_ctx):
    rows, d = h.shape
    tm = 256
    gw = GLA_HEADS * GLA_DV
    pwid = len(POOL_WINDOWS) * POOL_GC
    hb = tm // POOL_HALO
    n_hblk = rows // POOL_HALO
    r_col = (2 * GLA_HEADS * GLA_DK + gw) // gw
    u_col = (2 * GLA_HEADS * GLA_DK + 2 * gw) // pwid
    return pl.pallas_call(
        functools.partial(_mix0_out_kernel, n_lat=n_lat, n_ctx=n_ctx, tm=tm),
        grid=(rows // tm,),
        in_specs=[pl.BlockSpec((tm, d), lambda i: (i, 0)),
                  pl.BlockSpec((2, 3, d), lambda i: (0, 0, 0)),
                  pl.BlockSpec((tm, gw), lambda i: (i, 0)),
                  pl.BlockSpec((tm, gw), lambda i: (i, 0)),
                  pl.BlockSpec((tm, gw), lambda i: (i, r_col)),
                  pl.BlockSpec((POOL_HALO, pwid), lambda i: (jnp.maximum(i * hb - 1, 0), u_col)),
                  pl.BlockSpec((tm, pwid), lambda i: (i, u_col)),
                  pl.BlockSpec((POOL_HALO, pwid),
                               lambda i: (jnp.minimum((i + 1) * hb, n_hblk - 1), u_col)),
                  pl.BlockSpec((1, GLA_DV), lambda i: (0, 0)),
                  pl.BlockSpec((len(POOL_WINDOWS), POOL_GC, POOL_GC), lambda i: (0, 0, 0)),
                  pl.BlockSpec((1, pwid), lambda i: (0, 0)),
                  pl.BlockSpec((d, d), lambda i: (0, 0))],
        out_specs=pl.BlockSpec((tm, d), lambda i: (i, 0)),
        out_shape=jax.ShapeDtypeStruct((rows, d), F32),
        scratch_shapes=[pltpu.VMEM((tm + 2 * POOL_HALO, pwid), F32)],
        compiler_params=_cparams(("arbitrary",), 48),
        name="mix0_readout",
    )(h, mod3, o_f, o_b, p0, p0, p0, p0, norm_w.reshape(1, GLA_DV), pool_w,
      pool_scale.reshape(1, pwid), w_out)


def _proj1_kernel(h_ref, mod_ref, wqk_ref, wvt_ref, cos_ref, sin_ref, qk_ref, vt_ref, z_ref,
                  *, n_lat, tm, tn, n_qt, qscale):
    i = pl.program_id(0)
    j = pl.program_id(1)

    @pl.when(j == 0)
    def _():
        _store_modulated(h_ref, mod_ref, z_ref, i * tm, tm, n_lat)

    @pl.when(j < 2 * n_qt)
    def _():
        acc = jnp.dot(z_ref[...], wqk_ref[...], preferred_element_type=F32)
        scale = jnp.where(j < n_qt, qscale, 1.0)
        cs = cos_ref[...] * scale
        sn = sin_ref[...] * scale
        for gidx in range(tn // LANES):
            x = acc[:, gidx * LANES:(gidx + 1) * LANES]
            y = x * cs + pltpu.roll(x, LANES // 2, 1) * sn
            qk_ref[:, gidx * LANES:(gidx + 1) * LANES] = y.astype(BF16)

    @pl.when(j >= 2 * n_qt)
    def _():
        vt = lax.dot_general(wvt_ref[...], z_ref[...], (((1,), (1,)), ((), ())),
                             preferred_element_type=F32)
        for hd in range(tn // DIFF_DV):
            vt_ref[hd] = vt[hd * DIFF_DV:(hd + 1) * DIFF_DV].astype(BF16)


def _proj1(h, mod3, w_qk, w_vt, cos_t, sin_t, *, tm, n_lat):
    rows, d = h.shape
    tn = 512
    n_qt = d // tn
    n_vt = w_vt.shape[0] // tn
    hpt = tn // DIFF_DV
    qscale = (DIFF_DH ** -0.5) * math.log2(math.e)
    return pl.pallas_call(
        functools.partial(_proj1_kernel, n_lat=n_lat, tm=tm, tn=tn, n_qt=n_qt, qscale=qscale),
        grid=(rows // tm, 2 * n_qt + n_vt),
        in_specs=[pl.BlockSpec((tm, d), lambda i, j: (i, 0)),
                  pl.BlockSpec((2, 3, d), lambda i, j: (0, 0, 0)),
                  pl.BlockSpec((d, tn), lambda i, j: (0, jnp.minimum(j, 2 * n_qt - 1))),
                  pl.BlockSpec((tn, d), lambda i, j: (jnp.maximum(j - 2 * n_qt, 0), 0)),
                  pl.BlockSpec((tm, LANES), lambda i, j: (i, 0)),
                  pl.BlockSpec((tm, LANES), lambda i, j: (i, 0))],
        out_specs=[pl.BlockSpec((tm, tn), lambda i, j: (i, jnp.minimum(j, 2 * n_qt - 1))),
                   pl.BlockSpec((hpt, None, DIFF_DV, tm),
                                lambda i, j: (jnp.maximum(j - 2 * n_qt, 0), i, 0, 0))],
        out_shape=[jax.ShapeDtypeStruct((rows, 2 * d), BF16),
                   jax.ShapeDtypeStruct((DIFF_HEADS, rows // tm, DIFF_DV, tm), BF16)],
        scratch_shapes=[pltpu.VMEM((tm, d), BF16)],
        compiler_params=_cparams(("arbitrary", "arbitrary"), 40),
        name="proj_qkv_rope",
    )(h, mod3, w_qk, w_vt, cos_t, sin_t)


ATT_TILE = 256


def _attn_kernel(lam_ref, q_ref, k_ref, vt_ref, nw_ref, o_ref, m_ref, l_ref, acc_ref,
                 sa_ref, sb_ref, ca_ref, cb_ref, p_ref, *, tk, n_kv, lam_init):
    tq = q_ref.shape[0]
    T = ATT_TILE
    SUB = 8
    m_ref[...] = jnp.full_like(m_ref, -jnp.inf)
    l_ref[...] = jnp.zeros_like(l_ref)
    acc_ref[...] = jnp.zeros_like(acc_ref)

    def scores(jj, s_ref, c_ref):
        for c in range(2):
            comp = slice(c * DIFF_DH, (c + 1) * DIFF_DH)
            for qb in range(tq // T):
                cols = slice(qb * T, (qb + 1) * T)
                qc = q_ref[cols, comp]
                cmax = None
                for kb in range(tk // T):
                    kc = k_ref[jj * tk + kb * T:jj * tk + (kb + 1) * T, comp]
                    st = lax.dot_general(kc, qc, (((1,), (1,)), ((), ())),
                                         preferred_element_type=F32)
                    s_ref[c, kb * T:(kb + 1) * T, cols] = st
                    bmax = jnp.max(st.reshape(T // SUB, SUB, T), axis=0)
                    cmax = bmax if cmax is None else jnp.maximum(cmax, bmax)
                c_ref[c, :, cols] = cmax

    def accumulate(jj, s_ref, c_ref):
        for c in range(2):
            for qb in range(tq // T):
                cols = slice(qb * T, (qb + 1) * T)
                m_old = m_ref[c, :, cols]
                m_new = jnp.maximum(m_old, jnp.max(c_ref[c, :, cols], axis=0, keepdims=True))
                alpha = jnp.exp2(m_old - m_new)
                lsum = jnp.zeros((SUB, T), F32)
                for kb in range(tk // T):
                    keys = slice(kb * T, (kb + 1) * T)
                    pt = jnp.exp2(s_ref[c, keys, cols] - m_new)
                    lsum = lsum + jnp.sum(pt.reshape(T // SUB, SUB, T), axis=0)
                    p_ref[c, keys, cols] = pt.astype(BF16)
                pv = jnp.dot(vt_ref[jj], p_ref[c, :, cols], preferred_element_type=F32)
                l_ref[c, :, cols] = alpha * l_ref[c, :, cols] + lsum
                acc_ref[c, :, cols] = alpha * acc_ref[c, :, cols] + pv
                m_ref[c, :, cols] = m_new

    slots = ((sa_ref, ca_ref), (sb_ref, cb_ref))
    scores(0, *slots[0])
    for jj in range(n_kv):
        accumulate(jj, *slots[jj % 2])
        if jj + 1 < n_kv:
            scores(jj + 1, *slots[(jj + 1) % 2])

    lp = lam_ref[...]
    lam = (jnp.exp(jnp.sum(lp[0:1] * lp[1:2], axis=-1, keepdims=True))
           - jnp.exp(jnp.sum(lp[2:3] * lp[3:4], axis=-1, keepdims=True)) + lam_init)
    l0 = jnp.sum(l_ref[0], axis=0, keepdims=True)
    l1 = jnp.sum(l_ref[1], axis=0, keepdims=True)
    ot = acc_ref[0] / l0 - lam * (acc_ref[1] / l1)
    ms = jnp.mean(ot * ot, axis=0, keepdims=True)
    ont = ot * lax.rsqrt(ms + NORM_EPS) * (nw_ref[...] * (1.0 - lam_init))
    o_ref[...] = ont.T.astype(BF16)


def _diff_attn(qk, vt, lam_p, norm_w, *, n_lat, lam_init):
    rows = qk.shape[0]
    d = DIFF_HEADS * DIFF_DV
    n_kv, tk = vt.shape[1], vt.shape[3]
    tq = 512
    return pl.pallas_call(
        functools.partial(_attn_kernel, tk=tk, n_kv=n_kv, lam_init=lam_init),
        grid=(DIFF_HEADS, n_lat // tq),
        in_specs=[pl.BlockSpec((4, DIFF_DH), lambda h, i: (0, 0)),
                  pl.BlockSpec((tq, DIFF_DV), lambda h, i: (i, h)),
                  pl.BlockSpec((rows, DIFF_DV), lambda h, i: (0, DIFF_HEADS + h)),
                  pl.BlockSpec((None, n_kv, DIFF_DV, tk), lambda h, i: (h, 0, 0, 0)),
                  pl.BlockSpec((DIFF_DV, 1), lambda h, i: (0, 0))],
        out_specs=pl.BlockSpec((tq, DIFF_DV), lambda h, i: (i, h)),
        out_shape=jax.ShapeDtypeStruct((n_lat, d), BF16),
        scratch_shapes=[pltpu.VMEM((2, 1, tq), F32), pltpu.VMEM((2, 8, tq), F32),
                        pltpu.VMEM((2, DIFF_DV, tq), F32),
                        pltpu.VMEM((2, tk, tq), F32), pltpu.VMEM((2, tk, tq), F32),
                        pltpu.VMEM((2, 8, tq), F32), pltpu.VMEM((2, 8, tq), F32),
                        pltpu.VMEM((2, tk, tq), BF16)],
        compiler_params=_cparams(("arbitrary", "arbitrary"), 48),
        name="diff_attn",
    )(lam_p, qk, qk, vt, norm_w.reshape(DIFF_DV, 1))


def _outproj_kernel(h_ref, mod_ref, x_ref, w_ref, o_ref):
    y = jnp.dot(x_ref[...], w_ref[...], preferred_element_type=F32)
    o_ref[...] = h_ref[...] + mod_ref[0, 2:3, :] * y


def _outproj_latent(h, mod3, x, w, *, n_lat):
    d = h.shape[1]
    tm = 512
    return pl.pallas_call(
        _outproj_kernel,
        grid=(n_lat // tm,),
        in_specs=[pl.BlockSpec((tm, d), lambda i: (i, 0)),
                  pl.BlockSpec((2, 3, d), lambda i: (0, 0, 0)),
                  pl.BlockSpec((tm, d), lambda i: (i, 0)),
                  pl.BlockSpec((d, d), lambda i: (0, 0))],
        out_specs=pl.BlockSpec((tm, d), lambda i: (i, 0)),
        out_shape=jax.ShapeDtypeStruct((n_lat, d), F32),
        compiler_params=_cparams(("arbitrary",), 40),
        name="attn_outproj",
    )(h, mod3, x, w)


def _rope_tables(n_lat, n_ctx):
    rows = n_lat // GRID_W
    row = jnp.repeat(jnp.arange(rows), GRID_W).astype(F32)
    col = jnp.tile(jnp.arange(GRID_W), rows).astype(F32)
    n_freq = DIFF_DH // 4
    inv_freq = ROPE_THETA ** (-jnp.arange(n_freq, dtype=F32) / n_freq)
    ang = jnp.concatenate([row[:, None] * inv_freq, col[:, None] * inv_freq], axis=-1)
    cos, sin = jnp.cos(ang), jnp.sin(ang)
    cos_t = jnp.concatenate([cos, cos], axis=-1)
    sin_t = jnp.concatenate([-sin, sin], axis=-1)
    cos_t = jnp.concatenate([cos_t, jnp.ones((n_ctx, DIFF_DH), F32)], axis=0)
    sin_t = jnp.concatenate([sin_t, jnp.zeros((n_ctx, DIFF_DH), F32)], axis=0)
    return cos_t, sin_t


def kernel(x, c, ctx, c_ctx, ada_w, ada_b, ffn_w1, ffn_w3, ffn_w2, gla_w_in, gla_gate_w2,
           gla_gate_b, gla_norm_w, pool_w, pool_scale, mix0_w_out, diff_w_qkv, diff_lambda,
           diff_norm_w, diff_w_out, final_norm_w):
    assert x.shape[0] == 1 and ada_w.shape[0] == 2
    n_lat, d = x.shape[1], x.shape[2]
    n_ctx = ctx.shape[1]
    rows = n_lat + n_ctx
    tm = 768
    assert rows % tm == 0 and n_lat % 512 == 0 and n_ctx % 256 == 0 and n_lat % GRID_W == 0

    h = jnp.concatenate([x[0], ctx[0]], axis=0)
    c8 = jnp.zeros((8, d), F32).at[0].set(c[0]).at[1].set(c_ctx)
    mods = _ada(c8, ada_w, ada_b)[:, :2].reshape(2, 2, 9, d)

    def first_weights(layer, half):
        return tuple(wt[layer, half].astype(BF16) for wt in (ffn_w1, ffn_w3, ffn_w2))

    def following(layer, half):
        return (ffn_w1, ffn_w3, ffn_w2, layer, half)

    m = mods[0]
    h, w_next = _ffn(h, m[:, 0:3], first_weights(0, 0), rows=rows, tm=tm, n_lat=n_lat,
                     nxt=following(0, 1))
    w_in = gla_w_in[0]
    n_qkvr = 2 * GLA_HEADS * GLA_DK + 2 * GLA_HEADS * GLA_DV
    n_gate = 2 * GLA_GATE_RANK
    w_main = jnp.concatenate([w_in[:, :n_qkvr], w_in[:, n_qkvr + n_gate:]], axis=1).astype(BF16)
    w_gate = jnp.pad(w_in[:, n_qkvr:n_qkvr + n_gate], ((0, 0), (0, LANES - n_gate))).astype(BF16)
    gw2 = gla_gate_w2[0].astype(BF16)
    gw2p = jnp.zeros((2, LANES, gw2.shape[-1]), BF16)
    gw2p = gw2p.at[0, :GLA_GATE_RANK].set(gw2[0]).at[1, GLA_GATE_RANK:n_gate].set(gw2[1])
    p0, lf, lb = _proj0(h, m[:, 3:6], w_main, w_gate, gw2p, gla_gate_b[0], tm=tm, n_lat=n_lat)
    v_lo = 2 * GLA_HEADS * GLA_DK
    v16 = p0[:, v_lo:v_lo + GLA_HEADS * GLA_DV].astype(BF16)
    o_f, o_b = _gla(p0, v16, lf, lb, n_lat=n_lat)
    h = _mix0_out(h, m[:, 3:6], o_f, o_b, p0, gla_norm_w[0], pool_w[0].astype(BF16),
                  pool_scale[0], mix0_w_out[0].astype(BF16), n_lat=n_lat, n_ctx=n_ctx)
    h, w_next = _ffn(h, m[:, 6:9], w_next, rows=rows, tm=tm, n_lat=n_lat, nxt=following(1, 0))

    m = mods[1]
    h, w_next = _ffn(h, m[:, 0:3], w_next, rows=rows, tm=tm, n_lat=n_lat, nxt=following(1, 1))
    cos_t, sin_t = _rope_tables(n_lat, n_ctx)
    w_qk = diff_w_qkv[0, :, :2 * d].astype(BF16)
    w_vt = diff_w_qkv[0, :, 2 * d:].T.astype(BF16)
    qk, vt = _proj1(h, m[:, 3:6], w_qk, w_vt, cos_t, sin_t, tm=tm, n_lat=n_lat)
    lam_init = 0.8 - 0.6 * math.exp(-0.3 * 1)
    o = _diff_attn(qk, vt, diff_lambda[0], diff_norm_w[0], n_lat=n_lat, lam_init=lam_init)
    hl = _outproj_latent(h, m[:, 3:6], o, diff_w_out[0].astype(BF16), n_lat=n_lat)
    out = _ffn(hl, m[:, 6:9], w_next, rows=n_lat, tm=512, n_lat=n_lat, final_w=final_norm_w)
    return out[None]
```

```python
import functools
import math

import jax
import jax.numpy as jnp
from jax import lax
from jax.experimental import pallas as pl
from jax.experimental.pallas import tpu as pltpu

F32 = jnp.float32
BF16 = jnp.bfloat16

NORM_EPS = 1e-6
GRID_W = 64
ROPE_THETA = 10000.0

GLA_HEADS = 4
GLA_DK = 128
GLA_DV = 256
GLA_GATE_RANK = 16
GLA_TAU = 16.0
GLA_CHUNK = 64
GLA_SUB = 16
GLA_GROUP = 256
POOL_WINDOWS = (2, 4, 8, 16)
POOL_GC = 256
POOL_HALO = 8

DIFF_HEADS = 8
DIFF_DH = 128
DIFF_DV = 256

LANES = 128
MIB = 1024 * 1024


def _cparams(semantics, vmem_mib):
    return pltpu.CompilerParams(dimension_semantics=semantics,
                                vmem_limit_bytes=int(vmem_mib * MIB))


def _rms(x):
    return x * lax.rsqrt(jnp.mean(x * x, axis=-1, keepdims=True) + NORM_EPS)


def _is_ctx_rows(row0, rows, n_lat):
    return (row0 + lax.broadcasted_iota(jnp.int32, (rows, 1), 0)) >= n_lat


def _mod_row(mod_ref, is_ctx, k):
    return jnp.where(is_ctx, mod_ref[1, k:k + 1, :], mod_ref[0, k:k + 1, :])


def _modulated(h, mod_ref, is_ctx):
    return _rms(h) * (1.0 + _mod_row(mod_ref, is_ctx, 1)) + _mod_row(mod_ref, is_ctx, 0)


NORM_ROWS = 32


def _store_modulated(h_ref, mod_ref, z_ref, row0, tm, n_lat):
    def body(r, carry):
        start = pl.multiple_of(r * NORM_ROWS, NORM_ROWS)
        rows = pl.ds(start, NORM_ROWS)
        which = jnp.where(row0 + start >= n_lat, 1, 0)
        shift = mod_ref[which, 0:1, :]
        scale = mod_ref[which, 1:2, :]
        z_ref[rows, :] = (_rms(h_ref[rows, :]) * (1.0 + scale) + shift).astype(BF16)
        return carry

    lax.fori_loop(0, tm // NORM_ROWS, body, 0, unroll=4)


def _ada_kernel(c_ref, w_ref, b_ref, o_ref):
    c = c_ref[...]
    s = (c * jax.nn.sigmoid(c)).astype(BF16)
    o_ref[0] = jnp.dot(s, w_ref[0].astype(BF16), preferred_element_type=F32) + b_ref[0]


def _ada(c8, ada_w, ada_b):
    depth, d, n = ada_w.shape
    tn = 1024
    return pl.pallas_call(
        _ada_kernel,
        grid=(depth, n // tn),
        in_specs=[pl.BlockSpec((8, d), lambda i, j: (0, 0)),
                  pl.BlockSpec((1, d, tn), lambda i, j: (i, 0, j)),
                  pl.BlockSpec((1, 1, tn), lambda i, j: (i, 0, j))],
        out_specs=pl.BlockSpec((1, 8, tn), lambda i, j: (i, 0, j)),
        out_shape=jax.ShapeDtypeStruct((depth, 8, n), F32),
        compiler_params=_cparams(("arbitrary", "arbitrary"), 40),
        name="ada_mod",
    )(c8, ada_w, ada_b.reshape(depth, 1, n))


def _ffn_kernel(*refs, n_lat, tm, final, starts):
    if final:
        h_ref, mod_ref, w1_ref, w3_ref, w2_ref, fnw_ref, o_ref, z_ref = refs
        casts = ()
    else:
        (h_ref, mod_ref, w1_ref, w3_ref, w2_ref, n1_ref, n3_ref, n2_ref,
         o_ref, c1_ref, c3_ref, c2_ref, z_ref) = refs
        casts = ((n1_ref, c1_ref), (n3_ref, c3_ref), (n2_ref, c2_ref))
    i = pl.program_id(0)
    j = pl.program_id(1)
    is_ctx = _is_ctx_rows(i * tm, tm, n_lat)

    @pl.when(j == 0)
    def _():
        _store_modulated(h_ref, mod_ref, z_ref, i * tm, tm, n_lat)
        o_ref[...] = jnp.zeros_like(o_ref)

    step = i * pl.num_programs(1) + j
    for m, (src_ref, dst_ref) in enumerate(casts):
        @pl.when((step >= starts[m]) & (step < starts[m + 1]))
        def _(src_ref=src_ref, dst_ref=dst_ref):
            dst_ref[...] = src_ref[...].astype(BF16)

    z = z_ref[...]
    u = jnp.dot(z, w1_ref[...], preferred_element_type=F32)
    g = jnp.dot(z, w3_ref[...], preferred_element_type=F32)
    a = (u * jax.nn.sigmoid(u) * g).astype(BF16)
    o_ref[...] += jnp.dot(a, w2_ref[...], preferred_element_type=F32)

    @pl.when(j == pl.num_programs(1) - 1)
    def _():
        hn = h_ref[...] + 0.5 * _mod_row(mod_ref, is_ctx, 2) * o_ref[...]
        if final:
            hn = _rms(hn) * fnw_ref[...]
        o_ref[...] = hn


def _ffn(h, mod3, w, *, rows, tm, n_lat, nxt=None, final_w=None):
    w1, w3, w2 = w
    d = h.shape[1]
    dff = w1.shape[-1]
    tf = 512
    nj = dff // tf
    final = final_w is not None
    in_specs = [pl.BlockSpec((tm, d), lambda i, j: (i, 0)),
                pl.BlockSpec((2, 3, d), lambda i, j: (0, 0, 0)),
                pl.BlockSpec((d, tf), lambda i, j: (0, j)),
                pl.BlockSpec((d, tf), lambda i, j: (0, j)),
                pl.BlockSpec((tf, d), lambda i, j: (j, 0))]
    args = [h, mod3, w1, w3, w2]
    out_specs = [pl.BlockSpec((tm, d), lambda i, j: (i, 0))]
    out_shape = [jax.ShapeDtypeStruct((rows, d), F32)]
    starts = ()
    if final:
        in_specs.append(pl.BlockSpec((1, d), lambda i, j: (0, 0)))
        args.append(final_w.reshape(1, d))
    else:
        f1, f3, f2, layer, half = nxt
        cc = tf
        per_row = dff // cc
        steps = (rows // tm) * nj
        cr_up, cr_down = next((a, b) for a, b in ((d // 4, d // 2), (d // 2, d // 2), (d, d))
                              if (2 * (d // a) + d // b) * per_row <= steps)
        counts = ((d // cr_up) * per_row, (d // cr_up) * per_row, (d // cr_down) * per_row)
        starts = (0, counts[0], counts[0] + counts[1], sum(counts))

        def blk(i, j, m):
            return jnp.clip(i * nj + j - starts[m], 0, counts[m] - 1)

        def up_map(m):
            return lambda i, j: (blk(i, j, m) // per_row, blk(i, j, m) % per_row)

        def down_map(m):
            return lambda i, j: (blk(i, j, m) % per_row, blk(i, j, m) // per_row)

        def stacked(index_map):
            return lambda i, j: (layer, half) + index_map(i, j)

        in_specs += [pl.BlockSpec((None, None, cr_up, cc), stacked(up_map(0))),
                     pl.BlockSpec((None, None, cr_up, cc), stacked(up_map(1))),
                     pl.BlockSpec((None, None, cc, cr_down), stacked(down_map(2)))]
        args += [f1, f3, f2]
        out_specs += [pl.BlockSpec((cr_up, cc), up_map(0)), pl.BlockSpec((cr_up, cc), up_map(1)),
                      pl.BlockSpec((cc, cr_down), down_map(2))]
        out_shape += [jax.ShapeDtypeStruct((d, dff), BF16), jax.ShapeDtypeStruct((d, dff), BF16),
                      jax.ShapeDtypeStruct((dff, d), BF16)]
    res = pl.pallas_call(
        functools.partial(_ffn_kernel, n_lat=n_lat, tm=tm, final=final, starts=starts),
        grid=(rows // tm, nj),
        in_specs=in_specs,
        out_specs=out_specs,
        out_shape=out_shape,
        scratch_shapes=[pltpu.VMEM((tm, d), BF16)],
        compiler_params=_cparams(("arbitrary", "arbitrary"), 56 if final else 60),
        name="ffn_final" if final else "ffn_half",
    )(*args)
    return res[0] if final else (res[0], tuple(res[1:]))


def _log_sigmoid(x):
    return jnp.minimum(x, 0.0) - jnp.log1p(jnp.exp(-jnp.abs(x)))


def _proj0_kernel(h_ref, mod_ref, wa_ref, wu_ref, wg_ref, gw2_ref, gb_ref,
                  p_ref, v16_ref, lf_ref, lb_ref, z_ref, *, n_lat, tm, n_a, v_tiles):
    i = pl.program_id(0)
    j = pl.program_id(1)

    is_v = (j >= v_tiles[0]) & (j < v_tiles[1])

    @pl.when(j == 0)
    def _():
        _store_modulated(h_ref, mod_ref, z_ref, i * tm, tm, n_lat)
        z = z_ref[...]
        p_ref[...] = jnp.dot(z, wa_ref[...], preferred_element_type=F32)
        gz = jnp.dot(z, wg_ref[...], preferred_element_type=F32).astype(BF16)
        for d, out in ((0, lf_ref), (1, lb_ref)):
            zz = jnp.dot(gz, gw2_ref[d], preferred_element_type=F32) + gb_ref[d]
            out[...] = _log_sigmoid(zz) / GLA_TAU

    @pl.when((j > 0) & (j < n_a) & jnp.logical_not(is_v))
    def _():
        p_ref[...] = jnp.dot(z_ref[...], wa_ref[...], preferred_element_type=F32)

    @pl.when(is_v)
    def _():
        acc = jnp.dot(z_ref[...], wa_ref[...], preferred_element_type=F32)
        p_ref[...] = acc
        v16_ref[...] = acc.astype(BF16)

    @pl.when(j >= n_a)
    def _():
        p_ref[...] = jnp.dot(z_ref[...], wu_ref[...], preferred_element_type=F32)


def _proj0(h, mod3, w_a, w_u, w_gate, gate_w2p, gate_b, *, tm, n_lat):
    rows, d = h.shape
    qk = gate_b.shape[-1]
    tn = 512
    n_a, n_u = w_a.shape[1] // tn, w_u.shape[1] // tn
    nj = n_a + n_u
    v_lo, v_w = 2 * qk, GLA_HEADS * GLA_DV
    v_tiles = (v_lo // tn, (v_lo + v_w) // tn)

    def v_map(i, j):
        return (i, jnp.clip(j - v_tiles[0], 0, v_tiles[1] - v_tiles[0] - 1))

    return pl.pallas_call(
        functools.partial(_proj0_kernel, n_lat=n_lat, tm=tm, n_a=n_a, v_tiles=v_tiles),
        grid=(rows // tm, nj),
        in_specs=[pl.BlockSpec((tm, d), lambda i, j: (i, 0)),
                  pl.BlockSpec((2, 3, d), lambda i, j: (0, 0, 0)),
                  pl.BlockSpec((d, tn), lambda i, j: (0, jnp.minimum(j, n_a - 1))),
                  pl.BlockSpec((d, tn), lambda i, j: (0, jnp.maximum(j - n_a, 0))),
                  pl.BlockSpec((d, LANES), lambda i, j: (0, 0)),
                  pl.BlockSpec((2, LANES, qk), lambda i, j: (0, 0, 0)),
                  pl.BlockSpec((2, 1, qk), lambda i, j: (0, 0, 0))],
        out_specs=[pl.BlockSpec((tm, tn), lambda i, j: (i, j)),
                   pl.BlockSpec((tm, tn), v_map),
                   pl.BlockSpec((tm, qk), lambda i, j: (i, 0)),
                   pl.BlockSpec((tm, qk), lambda i, j: (i, 0))],
        out_shape=[jax.ShapeDtypeStruct((rows, nj * tn), F32),
                   jax.ShapeDtypeStruct((rows, v_w), BF16),
                   jax.ShapeDtypeStruct((rows, qk), F32),
                   jax.ShapeDtypeStruct((rows, qk), F32)],
        scratch_shapes=[pltpu.VMEM((tm, d), BF16)],
        compiler_params=_cparams(("arbitrary", "arbitrary"), 48),
        name="proj_gla_pool",
    )(h, mod3, w_a, w_u, w_gate, gate_w2p, gate_b.reshape(2, 1, qk))


def _split3(x):
    x1 = x.astype(BF16)
    r1 = x - x1.astype(F32)
    x2 = r1.astype(BF16)
    x3 = (r1 - x2.astype(F32)).astype(BF16)
    return x1, x2, x3


def _gla_chunk(q_ref, k_ref, v_ref, g_ref, o_ref, s_ref, b_ref, base, head, reverse):
    C, SB = GLA_CHUNK, GLA_SUB
    nb = C // SB
    rows = pl.ds(base, C)
    kcols = slice(head * GLA_DK, (head + 1) * GLA_DK)
    vcols = slice(head * GLA_DV, (head + 1) * GLA_DV)
    q = q_ref[rows, kcols] * (GLA_DK ** -0.5)
    k = k_ref[rows, kcols]
    v = v_ref[rows, vcols]
    g = g_ref[rows, kcols]

    ti = lax.broadcasted_iota(jnp.int32, (C, C), 0)
    si = lax.broadcasted_iota(jnp.int32, (C, C), 1)
    tri = jnp.where((si >= ti) if reverse else (si <= ti), 1.0, 0.0).astype(BF16)
    b = sum(jnp.dot(tri, part, preferred_element_type=F32) for part in _split3(g))
    yield

    b_ref[0] = b
    b_ref[1] = k
    s_old = s_ref[...]
    inter = jnp.dot((q * jnp.exp(b)).astype(BF16), s_old.astype(BF16),
                    preferred_element_type=F32)
    last = 0 if reverse else C - 1
    b_last = b_ref[0, last:last + 1, :]
    ke = (k * jnp.exp(b_last - b)).astype(BF16)
    upd = lax.dot_general(ke, v, (((0,), (0,)), ((), ())), preferred_element_type=F32)
    raws = []
    for i in range(nb):
        blk = slice(i * SB, (i + 1) * SB)
        edge = (i + 1) * SB if reverse else i * SB - 1
        has_off = (i < nb - 1) if reverse else (i > 0)
        if has_off:
            r_i = b_ref[0, edge:edge + 1, :]
            qe = (q[blk] * jnp.exp(b[blk] - r_i)).astype(BF16)
            ke_i = (k * jnp.exp(jnp.minimum(r_i - b, 0.0))).astype(BF16)
            raws.append(lax.dot_general(qe, ke_i, (((1,), (1,)), ((), ())),
                                        preferred_element_type=F32))
        else:
            raws.append(None)
    yield

    trow = lax.broadcasted_iota(jnp.int32, (SB, LANES), 0)
    ones = jnp.ones((LANES, LANES), BF16)
    dsums = []
    for i in range(nb):
        blk = slice(i * SB, (i + 1) * SB)
        q_i = q[blk]
        b_i = b[blk]
        terms = []
        for s in range(SB):
            row = i * SB + s
            k_s = b_ref[1, row:row + 1, :]
            b_s = b_ref[0, row:row + 1, :]
            keep = (trow <= s) if reverse else (trow >= s)
            terms.append(jnp.where(keep, q_i * k_s * jnp.exp(b_i - b_s), 0.0))
        stacked = jnp.concatenate(terms, axis=0).astype(BF16)
        dsums.append(jnp.dot(stacked, ones, preferred_element_type=F32))
    yield

    col = lax.broadcasted_iota(jnp.int32, (SB, C), 1)
    lane = lax.broadcasted_iota(jnp.int32, (SB, LANES), 1)
    a_rows = []
    for i in range(nb):
        diag = jnp.zeros((SB, LANES), F32)
        for s in range(SB):
            diag = jnp.where(lane == i * SB + s, dsums[i][s * SB:(s + 1) * SB], diag)
        a_i = diag[:, :C]
        if raws[i] is not None:
            off_mask = (col >= (i + 1) * SB) if reverse else (col < i * SB)
            a_i = a_i + jnp.where(off_mask, raws[i], 0.0)
        a_rows.append(a_i)
    attn = jnp.concatenate(a_rows, axis=0).astype(BF16)
    intra = jnp.dot(attn, v, preferred_element_type=F32)
    decay = jnp.broadcast_to(jnp.exp(b_last), (GLA_DK, GLA_DK)).T
    s_ref[...] = jnp.concatenate([decay, decay], axis=1) * s_old + upd
    yield

    o_ref[rows, vcols] = inter + intra
    yield


GLA_STAGES = 5


def _gla_kernel(qf_ref, kf_ref, vf_ref, gf_ref, qb_ref, kb_ref, vb_ref, gb_ref,
                of_ref, ob_ref, *scratch):
    n_chain = 2 * GLA_HEADS
    state, stage = scratch[:n_chain], scratch[n_chain:]

    @pl.when(pl.program_id(0) == 0)
    def _():
        for s_ref in state:
            s_ref[...] = jnp.zeros_like(s_ref)

    n_sub = GLA_GROUP // GLA_CHUNK

    def body(c, carry):
        fbase = pl.multiple_of(c * GLA_CHUNK, GLA_CHUNK)
        rbase = pl.multiple_of((n_sub - 1 - c) * GLA_CHUNK, GLA_CHUNK)
        chains = []
        for head in range(GLA_HEADS):
            f, r = head, GLA_HEADS + head
            chains.append(_gla_chunk(qf_ref, kf_ref, vf_ref, gf_ref, of_ref, state[f], stage[f],
                                     fbase, head, False))
            chains.append(_gla_chunk(qb_ref, kb_ref, vb_ref, gb_ref, ob_ref, state[r], stage[r],
                                     rbase, head, True))
        for _ in range(GLA_STAGES):
            for chain in chains:
                next(chain)
        return carry

    lax.fori_loop(0, n_sub, body, 0)


def _gla(p0, v16, lf, lb, *, n_lat):
    rows = p0.shape[0]
    n_grp = rows // GLA_GROUP
    lat_grp = n_lat // GLA_GROUP
    G = GLA_GROUP
    qk_w = GLA_HEADS * GLA_DK
    v_w = GLA_HEADS * GLA_DV

    def fwd(c):
        return (c + lat_grp) % n_grp

    def bwd(c):
        return n_grp - 1 - c

    def spec(width, order, col):
        return pl.BlockSpec((G, width), lambda c: (order(c), col))

    in_specs = []
    for order in (fwd, bwd):
        in_specs += [spec(qk_w, order, 0), spec(qk_w, order, 1), spec(v_w, order, 0),
                     spec(qk_w, order, 0)]
    out_specs = [spec(v_w, fwd, 0), spec(v_w, bwd, 0)]
    return pl.pallas_call(
        _gla_kernel,
        grid=(n_grp,),
        in_specs=in_specs,
        out_specs=out_specs,
        out_shape=[jax.ShapeDtypeStruct((rows, v_w), F32)] * 2,
        scratch_shapes=([pltpu.VMEM((GLA_DK, GLA_DV), F32)] * (2 * GLA_HEADS)
                        + [pltpu.VMEM((2, GLA_CHUNK, GLA_DK), F32)] * (2 * GLA_HEADS)),
        compiler_params=_cparams(("arbitrary",), 32),
        name="gla_scan",
    )(p0, p0, v16, lf, p0, p0, v16, lb)


def _mix0_out_kernel(h_ref, mod_ref, of_ref, ob_ref, r_ref, uprev_ref, u_ref, unext_ref,
                     nw_ref, pw_ref, ps_ref, wout_ref, o_ref, ext_ref,
                     *, n_lat, n_ctx, tm):
    i = pl.program_id(0)
    lat_tiles = n_lat // tm
    n_tiles = (n_lat + n_ctx) // tm
    in_ctx = i >= lat_tiles
    is_first = (i == 0) | (i == lat_tiles)
    is_last = (i == lat_tiles - 1) | (i == n_tiles - 1)

    o = of_ref[...] + ob_ref[...]
    r = r_ref[...]
    heads = []
    for hd in range(GLA_HEADS):
        heads.append(_rms(o[:, hd * GLA_DV:(hd + 1) * GLA_DV]) * nw_ref[...])
    gl = jnp.concatenate(heads, axis=1) * (r * jax.nn.sigmoid(r))

    u = u_ref[...]
    H = POOL_HALO
    ext_ref[0:H, :] = jnp.where(is_first, 0.0, uprev_ref[...])
    ext_ref[H:H + tm, :] = u
    ext_ref[H + tm:2 * H + tm, :] = jnp.where(is_last, 0.0, unext_ref[...])
    seq_len = jnp.where(in_ctx, n_ctx, n_lat)
    t = (i - jnp.where(in_ctx, lat_tiles, 0)) * tm + lax.broadcasted_iota(jnp.int32, (tm, 1), 0)
    pooled = []
    for gi, w in enumerate(POOL_WINDOWS):
        cols = slice(gi * POOL_GC, (gi + 1) * POOL_GC)
        s = ext_ref[H - w // 2:H - w // 2 + tm, cols]
        for dlt in range(-w // 2 + 1, w - w // 2):
            s = s + ext_ref[H + dlt:H + dlt + tm, cols]
        lo = jnp.maximum(t - w // 2, 0)
        hi = jnp.minimum(t + (w - w // 2), seq_len)
        cnt = (hi - lo).astype(F32)
        pg = (s / cnt - u[:, cols]).astype(BF16)
        pooled.append(jnp.dot(pg, pw_ref[gi], preferred_element_type=F32) * ps_ref[:, cols])

    mix = jnp.concatenate([gl] + pooled, axis=1).astype(BF16)
    y = jnp.dot(mix, wout_ref[...], preferred_element_type=F32)
    is_ctx = _is_ctx_rows(i * tm, tm, n_lat)
    o_ref[...] = h_ref[...] + _mod_row(mod_ref, is_ctx, 2) * y


def _mix0_out(h, mod3, o_f, o_b, p0, norm_w, pool_w, pool_scale, w_out, *, n_lat, n_ctx):
    rows, d = h.shape
    tm = 256
    gw = GLA_HEADS * GLA_DV
    pwid = len(POOL_WINDOWS) * POOL_GC
    hb = tm // POOL_HALO
    n_hblk = rows // POOL_HALO
    r_col = (2 * GLA_HEADS * GLA_DK + gw) // gw
    u_col = (2 * GLA_HEADS * GLA_DK + 2 * gw) // pwid
    return pl.pallas_call(
        functools.partial(_mix0_out_kernel, n_lat=n_lat, n_ctx=n_ctx, tm=tm),
        grid=(rows // tm,),
        in_specs=[pl.BlockSpec((tm, d), lambda i: (i, 0)),
                  pl.BlockSpec((2, 3, d), lambda i: (0, 0, 0)),
                  pl.BlockSpec((tm, gw), lambda i: (i, 0)),
                  pl.BlockSpec((tm, gw), lambda i: (i, 0)),
                  pl.BlockSpec((tm, gw), lambda i: (i, r_col)),
                  pl.BlockSpec((POOL_HALO, pwid), lambda i: (jnp.maximum(i * hb - 1, 0), u_col)),
                  pl.BlockSpec((tm, pwid), lambda i: (i, u_col)),
                  pl.BlockSpec((POOL_HALO, pwid),
                               lambda i: (jnp.minimum((i + 1) * hb, n_hblk - 1), u_col)),
                  pl.BlockSpec((1, GLA_DV), lambda i: (0, 0)),
                  pl.BlockSpec((len(POOL_WINDOWS), POOL_GC, POOL_GC), lambda i: (0, 0, 0)),
                  pl.BlockSpec((1, pwid), lambda i: (0, 0)),
                  pl.BlockSpec((d, d), lambda i: (0, 0))],
        out_specs=pl.BlockSpec((tm, d), lambda i: (i, 0)),
        out_shape=jax.ShapeDtypeStruct((rows, d), F32),
        scratch_shapes=[pltpu.VMEM((tm + 2 * POOL_HALO, pwid), F32)],
        compiler_params=_cparams(("arbitrary",), 48),
        name="mix0_readout",
    )(h, mod3, o_f, o_b, p0, p0, p0, p0, norm_w.reshape(1, GLA_DV), pool_w,
      pool_scale.reshape(1, pwid), w_out)


def _proj1_kernel(h_ref, mod_ref, wqk_ref, wvt_ref, cos_ref, sin_ref, qk_ref, vt_ref, z_ref,
                  *, n_lat, tm, tn, n_qt, qscale):
    i = pl.program_id(0)
    j = pl.program_id(1)

    @pl.when(j == 0)
    def _():
        _store_modulated(h_ref, mod_ref, z_ref, i * tm, tm, n_lat)

    @pl.when(j < 2 * n_qt)
    def _():
        acc = jnp.dot(z_ref[...], wqk_ref[...], preferred_element_type=F32)
        scale = jnp.where(j < n_qt, qscale, 1.0)
        cs = cos_ref[...] * scale
        sn = sin_ref[...] * scale
        for gidx in range(tn // LANES):
            x = acc[:, gidx * LANES:(gidx + 1) * LANES]
            y = x * cs + pltpu.roll(x, LANES // 2, 1) * sn
            qk_ref[:, gidx * LANES:(gidx + 1) * LANES] = y.astype(BF16)

    @pl.when(j >= 2 * n_qt)
    def _():
        vt = lax.dot_general(wvt_ref[...], z_ref[...], (((1,), (1,)), ((), ())),
                             preferred_element_type=F32)
        for hd in range(tn // DIFF_DV):
            vt_ref[hd] = vt[hd * DIFF_DV:(hd + 1) * DIFF_DV].astype(BF16)


def _proj1(h, mod3, w_qk, w_vt, cos_t, sin_t, *, tm, n_lat):
    rows, d = h.shape
    tn = 1024
    n_qt = d // tn
    n_vt = w_vt.shape[0] // tn
    hpt = tn // DIFF_DV
    qscale = (DIFF_DH ** -0.5) * math.log2(math.e)
    return pl.pallas_call(
        functools.partial(_proj1_kernel, n_lat=n_lat, tm=tm, tn=tn, n_qt=n_qt, qscale=qscale),
        grid=(rows // tm, 2 * n_qt + n_vt),
        in_specs=[pl.BlockSpec((tm, d), lambda i, j: (i, 0)),
                  pl.BlockSpec((2, 3, d), lambda i, j: (0, 0, 0)),
                  pl.BlockSpec((d, tn), lambda i, j: (0, jnp.minimum(j, 2 * n_qt - 1))),
                  pl.BlockSpec((tn, d), lambda i, j: (jnp.maximum(j - 2 * n_qt, 0), 0)),
                  pl.BlockSpec((tm, LANES), lambda i, j: (i, 0)),
                  pl.BlockSpec((tm, LANES), lambda i, j: (i, 0))],
        out_specs=[pl.BlockSpec((tm, tn), lambda i, j: (i, jnp.minimum(j, 2 * n_qt - 1))),
                   pl.BlockSpec((hpt, None, DIFF_DV, tm),
                                lambda i, j: (jnp.maximum(j - 2 * n_qt, 0), i, 0, 0))],
        out_shape=[jax.ShapeDtypeStruct((rows, 2 * d), BF16),
                   jax.ShapeDtypeStruct((DIFF_HEADS, rows // tm, DIFF_DV, tm), BF16)],
        scratch_shapes=[pltpu.VMEM((tm, d), BF16)],
        compiler_params=_cparams(("arbitrary", "arbitrary"), 52),
        name="proj_qkv_rope",
    )(h, mod3, w_qk, w_vt, cos_t, sin_t)


ATT_TILE = 256


def _attn_kernel(lam_ref, q_ref, k_ref, vt_ref, nw_ref, o_ref, m_ref, l_ref, acc_ref,
                 sa_ref, sb_ref, ca_ref, cb_ref, p_ref, *, tk, n_kv, lam_init):
    tq = q_ref.shape[0]
    T = ATT_TILE
    SUB = 8
    m_ref[...] = jnp.full_like(m_ref, -jnp.inf)
    l_ref[...] = jnp.zeros_like(l_ref)
    acc_ref[...] = jnp.zeros_like(acc_ref)

    units = [(c, qb) for c in range(2) for qb in range(tq // T)]

    def scores(jj, s_ref, c_ref, c, qb):
        comp = slice(c * DIFF_DH, (c + 1) * DIFF_DH)
        cols = slice(qb * T, (qb + 1) * T)
        qc = q_ref[cols, comp]
        cmax = None
        for kb in range(tk // T):
            kc = k_ref[jj * tk + kb * T:jj * tk + (kb + 1) * T, comp]
            st = lax.dot_general(kc, qc, (((1,), (1,)), ((), ())), preferred_element_type=F32)
            s_ref[c, kb * T:(kb + 1) * T, cols] = st
            bmax = jnp.max(st.reshape(T // SUB, SUB, T), axis=0)
            cmax = bmax if cmax is None else jnp.maximum(cmax, bmax)
        c_ref[c, :, cols] = cmax

    def accumulate(jj, s_ref, c_ref, c, qb):
        cols = slice(qb * T, (qb + 1) * T)
        m_old = m_ref[c, :, cols]
        m_new = jnp.maximum(m_old, jnp.max(c_ref[c, :, cols], axis=0, keepdims=True))
        alpha = jnp.exp2(m_old - m_new)
        lsum = jnp.zeros((SUB, T), F32)
        for kb in range(tk // T):
            keys = slice(kb * T, (kb + 1) * T)
            pt = jnp.exp2(s_ref[c, keys, cols] - m_new)
            lsum = lsum + jnp.sum(pt.reshape(T // SUB, SUB, T), axis=0)
            p_ref[c, keys, cols] = pt.astype(BF16)
        pv = jnp.dot(vt_ref[jj], p_ref[c, :, cols], preferred_element_type=F32)
        l_ref[c, :, cols] = alpha * l_ref[c, :, cols] + lsum
        acc_ref[c, :, cols] = alpha * acc_ref[c, :, cols] + pv
        m_ref[c, :, cols] = m_new

    slots = ((sa_ref, ca_ref), (sb_ref, cb_ref))
    for c, qb in units:
        scores(0, *slots[0], c, qb)
    for jj in range(n_kv):
        for c, qb in units:
            accumulate(jj, *slots[jj % 2], c, qb)
            if jj + 1 < n_kv:
                scores(jj + 1, *slots[(jj + 1) % 2], c, qb)

    lp = lam_ref[...]
    lam = (jnp.exp(jnp.sum(lp[0:1] * lp[1:2], axis=-1, keepdims=True))
           - jnp.exp(jnp.sum(lp[2:3] * lp[3:4], axis=-1, keepdims=True)) + lam_init)
    l0 = jnp.sum(l_ref[0], axis=0, keepdims=True)
    l1 = jnp.sum(l_ref[1], axis=0, keepdims=True)
    ot = acc_ref[0] / l0 - lam * (acc_ref[1] / l1)
    ms = jnp.mean(ot * ot, axis=0, keepdims=True)
    ont = ot * lax.rsqrt(ms + NORM_EPS) * (nw_ref[...] * (1.0 - lam_init))
    o_ref[...] = ont.T.astype(BF16)


def _diff_attn(qk, vt, lam_p, norm_w, *, n_lat, lam_init):
    rows = qk.shape[0]
    d = DIFF_HEADS * DIFF_DV
    n_kv, tk = vt.shape[1], vt.shape[3]
    tq = 512
    return pl.pallas_call(
        functools.partial(_attn_kernel, tk=tk, n_kv=n_kv, lam_init=lam_init),
        grid=(DIFF_HEADS, n_lat // tq),
        in_specs=[pl.BlockSpec((4, DIFF_DH), lambda h, i: (0, 0)),
                  pl.BlockSpec((tq, DIFF_DV), lambda h, i: (i, h)),
                  pl.BlockSpec((rows, DIFF_DV), lambda h, i: (0, DIFF_HEADS + h)),
                  pl.BlockSpec((None, n_kv, DIFF_DV, tk), lambda h, i: (h, 0, 0, 0)),
                  pl.BlockSpec((DIFF_DV, 1), lambda h, i: (0, 0))],
        out_specs=pl.BlockSpec((tq, DIFF_DV), lambda h, i: (i, h)),
        out_shape=jax.ShapeDtypeStruct((n_lat, d), BF16),
        scratch_shapes=[pltpu.VMEM((2, 1, tq), F32), pltpu.VMEM((2, 8, tq), F32),
                        pltpu.VMEM((2, DIFF_DV, tq), F32),
                        pltpu.VMEM((2, tk, tq), F32), pltpu.VMEM((2, tk, tq), F32),
                        pltpu.VMEM((2, 8, tq), F32), pltpu.VMEM((2, 8, tq), F32),
                        pltpu.VMEM((2, tk, tq), BF16)],
        compiler_params=_cparams(("arbitrary", "arbitrary"), 48),
        name="diff_attn",
    )(lam_p, qk, qk, vt, norm_w.reshape(DIFF_DV, 1))


def _outproj_kernel(h_ref, mod_ref, x_ref, w_ref, o_ref):
    y = jnp.dot(x_ref[...], w_ref[...], preferred_element_type=F32)
    o_ref[...] = h_ref[...] + mod_ref[0, 2:3, :] * y


def _outproj_latent(h, mod3, x, w, *, n_lat):
    d = h.shape[1]
    tm = 512
    return pl.pallas_call(
        _outproj_kernel,
        grid=(n_lat // tm,),
        in_specs=[pl.BlockSpec((tm, d), lambda i: (i, 0)),
                  pl.BlockSpec((2, 3, d), lambda i: (0, 0, 0)),
                  pl.BlockSpec((tm, d), lambda i: (i, 0)),
                  pl.BlockSpec((d, d), lambda i: (0, 0))],
        out_specs=pl.BlockSpec((tm, d), lambda i: (i, 0)),
        out_shape=jax.ShapeDtypeStruct((n_lat, d), F32),
        compiler_params=_cparams(("arbitrary",), 40),
        name="attn_outproj",
    )(h, mod3, x, w)


def _rope_tables(n_lat, n_ctx):
    rows = n_lat // GRID_W
    row = jnp.repeat(jnp.arange(rows), GRID_W).astype(F32)
    col = jnp.tile(jnp.arange(GRID_W), rows).astype(F32)
    n_freq = DIFF_DH // 4
    inv_freq = ROPE_THETA ** (-jnp.arange(n_freq, dtype=F32) / n_freq)
    ang = jnp.concatenate([row[:, None] * inv_freq, col[:, None] * inv_freq], axis=-1)
    cos, sin = jnp.cos(ang), jnp.sin(ang)
    cos_t = jnp.concatenate([cos, cos], axis=-1)
    sin_t = jnp.concatenate([-sin, sin], axis=-1)
    cos_t = jnp.concatenate([cos_t, jnp.ones((n_ctx, DIFF_DH), F32)], axis=0)
    sin_t = jnp.concatenate([sin_t, jnp.zeros((n_ctx, DIFF_DH), F32)], axis=0)
    return cos_t, sin_t


def kernel(x, c, ctx, c_ctx, ada_w, ada_b, ffn_w1, ffn_w3, ffn_w2, gla_w_in, gla_gate_w2,
           gla_gate_b, gla_norm_w, pool_w, pool_scale, mix0_w_out, diff_w_qkv, diff_lambda,
           diff_norm_w, diff_w_out, final_norm_w):
    assert x.shape[0] == 1 and ada_w.shape[0] == 2
    n_lat, d = x.shape[1], x.shape[2]
    n_ctx = ctx.shape[1]
    rows = n_lat + n_ctx
    tm = 768
    assert rows % tm == 0 and n_lat % 512 == 0 and n_ctx % 256 == 0 and n_lat % GRID_W == 0

    h = jnp.concatenate([x[0], ctx[0]], axis=0)
    c8 = jnp.zeros((8, d), F32).at[0].set(c[0]).at[1].set(c_ctx)
    mods = _ada(c8, ada_w, ada_b)[:, :2].reshape(2, 2, 9, d)

    def first_weights(layer, half):
        return tuple(wt[layer, half].astype(BF16) for wt in (ffn_w1, ffn_w3, ffn_w2))

    def following(layer, half):
        return (ffn_w1, ffn_w3, ffn_w2, layer, half)

    m = mods[0]
    h, w_next = _ffn(h, m[:, 0:3], first_weights(0, 0), rows=rows, tm=tm, n_lat=n_lat,
                     nxt=following(0, 1))
    w_in = gla_w_in[0]
    n_qkvr = 2 * GLA_HEADS * GLA_DK + 2 * GLA_HEADS * GLA_DV
    n_gate = 2 * GLA_GATE_RANK
    w_a = w_in[:, :n_qkvr].astype(BF16)
    w_u = w_in[:, n_qkvr + n_gate:].astype(BF16)
    w_gate = jnp.pad(w_in[:, n_qkvr:n_qkvr + n_gate], ((0, 0), (0, LANES - n_gate))).astype(BF16)
    gw2 = gla_gate_w2[0].astype(BF16)
    gw2p = jnp.zeros((2, LANES, gw2.shape[-1]), BF16)
    gw2p = gw2p.at[0, :GLA_GATE_RANK].set(gw2[0]).at[1, GLA_GATE_RANK:n_gate].set(gw2[1])
    p0, v16, lf, lb = _proj0(h, m[:, 3:6], w_a, w_u, w_gate, gw2p, gla_gate_b[0], tm=tm,
                             n_lat=n_lat)
    o_f, o_b = _gla(p0, v16, lf, lb, n_lat=n_lat)
    h = _mix0_out(h, m[:, 3:6], o_f, o_b, p0, gla_norm_w[0], pool_w[0].astype(BF16),
                  pool_scale[0], mix0_w_out[0].astype(BF16), n_lat=n_lat, n_ctx=n_ctx)
    h, w_next = _ffn(h, m[:, 6:9], w_next, rows=rows, tm=tm, n_lat=n_lat, nxt=following(1, 0))

    m = mods[1]
    h, w_next = _ffn(h, m[:, 0:3], w_next, rows=rows, tm=tm, n_lat=n_lat, nxt=following(1, 1))
    cos_t, sin_t = _rope_tables(n_lat, n_ctx)
    w_qk = diff_w_qkv[0, :, :2 * d].astype(BF16)
    w_vt = diff_w_qkv[0, :, 2 * d:].T.astype(BF16)
    qk, vt = _proj1(h, m[:, 3:6], w_qk, w_vt, cos_t, sin_t, tm=tm, n_lat=n_lat)
    lam_init = 0.8 - 0.6 * math.exp(-0.3 * 1)
    o = _diff_attn(qk, vt, diff_lambda[0], diff_norm_w[0], n_lat=n_lat, lam_init=lam_init)
    hl = _outproj_latent(h, m[:, 3:6], o, diff_w_out[0].astype(BF16), n_lat=n_lat)
    out = _ffn(hl, m[:, 6:9], w_next, rows=n_lat, tm=512, n_lat=n_lat, final_w=final_norm_w)
    return out[None]
```

```python
import functools
import math

import jax
import jax.numpy as jnp
from jax import lax
from jax.experimental import pallas as pl
from jax.experimental.pallas import tpu as pltpu

F32 = jnp.float32
BF16 = jnp.bfloat16

NORM_EPS = 1e-6
GRID_W = 64
ROPE_THETA = 10000.0

GLA_HEADS = 4
GLA_DK = 128
GLA_DV = 256
GLA_GATE_RANK = 16
GLA_TAU = 16.0
GLA_CHUNK = 64
GLA_SUB = 16
GLA_GROUP = 256
POOL_WINDOWS = (2, 4, 8, 16)
POOL_GC = 256
POOL_HALO = 8

DIFF_HEADS = 8
DIFF_DH = 128
DIFF_DV = 256

LANES = 128
MIB = 1024 * 1024


def _cparams(semantics, vmem_mib):
    return pltpu.CompilerParams(dimension_semantics=semantics,
                                vmem_limit_bytes=int(vmem_mib * MIB))


def _rms(x):
    return x * lax.rsqrt(jnp.mean(x * x, axis=-1, keepdims=True) + NORM_EPS)


def _is_ctx_rows(row0, rows, n_lat):
    return (row0 + lax.broadcasted_iota(jnp.int32, (rows, 1), 0)) >= n_lat


def _mod_row(mod_ref, is_ctx, k):
    return jnp.where(is_ctx, mod_ref[1, k:k + 1, :], mod_ref[0, k:k + 1, :])


def _modulated(h, mod_ref, is_ctx):
    return _rms(h) * (1.0 + _mod_row(mod_ref, is_ctx, 1)) + _mod_row(mod_ref, is_ctx, 0)


NORM_ROWS = 32


def _store_modulated(h_ref, mod_ref, z_ref, row0, tm, n_lat):
    def body(r, carry):
        start = pl.multiple_of(r * NORM_ROWS, NORM_ROWS)
        rows = pl.ds(start, NORM_ROWS)
        which = jnp.where(row0 + start >= n_lat, 1, 0)
        shift = mod_ref[which, 0:1, :]
        scale = mod_ref[which, 1:2, :]
        z_ref[rows, :] = (_rms(h_ref[rows, :]) * (1.0 + scale) + shift).astype(BF16)
        return carry

    lax.fori_loop(0, tm // NORM_ROWS, body, 0, unroll=4)


def _ada_kernel(c_ref, w_ref, b_ref, o_ref):
    c = c_ref[...]
    s = (c * jax.nn.sigmoid(c)).astype(BF16)
    o_ref[0] = jnp.dot(s, w_ref[0].astype(BF16), preferred_element_type=F32) + b_ref[0]


def _ada(c8, ada_w, ada_b):
    depth, d, n = ada_w.shape
    tn = 1024
    return pl.pallas_call(
        _ada_kernel,
        grid=(depth, n // tn),
        in_specs=[pl.BlockSpec((8, d), lambda i, j: (0, 0)),
                  pl.BlockSpec((1, d, tn), lambda i, j: (i, 0, j)),
                  pl.BlockSpec((1, 1, tn), lambda i, j: (i, 0, j))],
        out_specs=pl.BlockSpec((1, 8, tn), lambda i, j: (i, 0, j)),
        out_shape=jax.ShapeDtypeStruct((depth, 8, n), F32),
        compiler_params=_cparams(("arbitrary", "arbitrary"), 40),
        name="ada_mod",
    )(c8, ada_w, ada_b.reshape(depth, 1, n))


def _ffn_kernel(*refs, n_lat, tm, final, starts):
    if final:
        h_ref, mod_ref, w1_ref, w3_ref, w2_ref, fnw_ref, o_ref, z_ref = refs
        casts = ()
    else:
        (h_ref, mod_ref, w1_ref, w3_ref, w2_ref, n1_ref, n3_ref, n2_ref,
         o_ref, c1_ref, c3_ref, c2_ref, z_ref) = refs
        casts = ((n1_ref, c1_ref), (n3_ref, c3_ref), (n2_ref, c2_ref))
    i = pl.program_id(0)
    j = pl.program_id(1)
    is_ctx = _is_ctx_rows(i * tm, tm, n_lat)

    @pl.when(j == 0)
    def _():
        _store_modulated(h_ref, mod_ref, z_ref, i * tm, tm, n_lat)
        o_ref[...] = jnp.zeros_like(o_ref)

    step = i * pl.num_programs(1) + j
    for m, (src_ref, dst_ref) in enumerate(casts):
        @pl.when((step >= starts[m]) & (step < starts[m + 1]))
        def _(src_ref=src_ref, dst_ref=dst_ref):
            dst_ref[...] = src_ref[...].astype(BF16)

    z = z_ref[...]
    u = jnp.dot(z, w1_ref[...], preferred_element_type=F32)
    g = jnp.dot(z, w3_ref[...], preferred_element_type=F32)
    a = (u * jax.nn.sigmoid(u) * g).astype(BF16)
    o_ref[...] += jnp.dot(a, w2_ref[...], preferred_element_type=F32)

    @pl.when(j == pl.num_programs(1) - 1)
    def _():
        hn = h_ref[...] + 0.5 * _mod_row(mod_ref, is_ctx, 2) * o_ref[...]
        if final:
            hn = _rms(hn) * fnw_ref[...]
        o_ref[...] = hn


def _ffn(h, mod3, w, *, rows, tm, n_lat, nxt=None, final_w=None):
    w1, w3, w2 = w
    d = h.shape[1]
    dff = w1.shape[-1]
    tf = 512
    nj = dff // tf
    final = final_w is not None
    in_specs = [pl.BlockSpec((tm, d), lambda i, j: (i, 0)),
                pl.BlockSpec((2, 3, d), lambda i, j: (0, 0, 0)),
                pl.BlockSpec((d, tf), lambda i, j: (0, j)),
                pl.BlockSpec((d, tf), lambda i, j: (0, j)),
                pl.BlockSpec((tf, d), lambda i, j: (j, 0))]
    args = [h, mod3, w1, w3, w2]
    out_specs = [pl.BlockSpec((tm, d), lambda i, j: (i, 0))]
    out_shape = [jax.ShapeDtypeStruct((rows, d), F32)]
    starts = ()
    if final:
        in_specs.append(pl.BlockSpec((1, d), lambda i, j: (0, 0)))
        args.append(final_w.reshape(1, d))
    else:
        f1, f3, f2, layer, half = nxt
        cc = tf
        per_row = dff // cc
        steps = (rows // tm) * nj
        cr_up, cr_down = next((a, b) for a, b in ((d // 4, d // 2), (d // 2, d // 2), (d, d))
                              if (2 * (d // a) + d // b) * per_row <= steps)
        counts = ((d // cr_up) * per_row, (d // cr_up) * per_row, (d // cr_down) * per_row)
        starts = (0, counts[0], counts[0] + counts[1], sum(counts))

        def blk(i, j, m):
            return jnp.clip(i * nj + j - starts[m], 0, counts[m] - 1)

        def up_map(m):
            return lambda i, j: (blk(i, j, m) // per_row, blk(i, j, m) % per_row)

        def down_map(m):
            return lambda i, j: (blk(i, j, m) % per_row, blk(i, j, m) // per_row)

        def stacked(index_map):
            return lambda i, j: (layer, half) + index_map(i, j)

        in_specs += [pl.BlockSpec((None, None, cr_up, cc), stacked(up_map(0))),
                     pl.BlockSpec((None, None, cr_up, cc), stacked(up_map(1))),
                     pl.BlockSpec((None, None, cc, cr_down), stacked(down_map(2)))]
        args += [f1, f3, f2]
        out_specs += [pl.BlockSpec((cr_up, cc), up_map(0)), pl.BlockSpec((cr_up, cc), up_map(1)),
                      pl.BlockSpec((cc, cr_down), down_map(2))]
        out_shape += [jax.ShapeDtypeStruct((d, dff), BF16), jax.ShapeDtypeStruct((d, dff), BF16),
                      jax.ShapeDtypeStruct((dff, d), BF16)]
    res = pl.pallas_call(
        functools.partial(_ffn_kernel, n_lat=n_lat, tm=tm, final=final, starts=starts),
        grid=(rows // tm, nj),
        in_specs=in_specs,
        out_specs=out_specs,
        out_shape=out_shape,
        scratch_shapes=[pltpu.VMEM((tm, d), BF16)],
        compiler_params=_cparams(("arbitrary", "arbitrary"), 56 if final else 60),
        name="ffn_final" if final else "ffn_half",
    )(*args)
    return res[0] if final else (res[0], tuple(res[1:]))


def _log_sigmoid(x):
    return jnp.minimum(x, 0.0) - jnp.log1p(jnp.exp(-jnp.abs(x)))


def _proj0_kernel(h_ref, mod_ref, wa_ref, wu_ref, wg_ref, gw2_ref, gb_ref,
                  p_ref, v16_ref, lf_ref, lb_ref, z_ref, *, n_lat, tm, n_a, v_tiles):
    i = pl.program_id(0)
    j = pl.program_id(1)

    is_v = (j >= v_tiles[0]) & (j < v_tiles[1])

    @pl.when(j == 0)
    def _():
        _store_modulated(h_ref, mod_ref, z_ref, i * tm, tm, n_lat)
        z = z_ref[...]
        p_ref[...] = jnp.dot(z, wa_ref[...], preferred_element_type=F32)
        gz = jnp.dot(z, wg_ref[...], preferred_element_type=F32).astype(BF16)
        for d, out in ((0, lf_ref), (1, lb_ref)):
            zz = jnp.dot(gz, gw2_ref[d], preferred_element_type=F32) + gb_ref[d]
            out[...] = _log_sigmoid(zz) / GLA_TAU

    @pl.when((j > 0) & (j < n_a) & jnp.logical_not(is_v))
    def _():
        p_ref[...] = jnp.dot(z_ref[...], wa_ref[...], preferred_element_type=F32)

    @pl.when(is_v)
    def _():
        acc = jnp.dot(z_ref[...], wa_ref[...], preferred_element_type=F32)
        p_ref[...] = acc
        v16_ref[...] = acc.astype(BF16)

    @pl.when(j >= n_a)
    def _():
        p_ref[...] = jnp.dot(z_ref[...], wu_ref[...], preferred_element_type=F32)


def _proj0(h, mod3, w_a, w_u, w_gate, gate_w2p, gate_b, *, tm, n_lat):
    rows, d = h.shape
    qk = gate_b.shape[-1]
    tn = 512
    n_a, n_u = w_a.shape[1] // tn, w_u.shape[1] // tn
    nj = n_a + n_u
    v_lo, v_w = 2 * qk, GLA_HEADS * GLA_DV
    v_tiles = (v_lo // tn, (v_lo + v_w) // tn)

    def v_map(i, j):
        return (i, jnp.clip(j - v_tiles[0], 0, v_tiles[1] - v_tiles[0] - 1))

    return pl.pallas_call(
        functools.partial(_proj0_kernel, n_lat=n_lat, tm=tm, n_a=n_a, v_tiles=v_tiles),
        grid=(rows // tm, nj),
        in_specs=[pl.BlockSpec((tm, d), lambda i, j: (i, 0)),
                  pl.BlockSpec((2, 3, d), lambda i, j: (0, 0, 0)),
                  pl.BlockSpec((d, tn), lambda i, j: (0, jnp.minimum(j, n_a - 1))),
                  pl.BlockSpec((d, tn), lambda i, j: (0, jnp.maximum(j - n_a, 0))),
                  pl.BlockSpec((d, LANES), lambda i, j: (0, 0)),
                  pl.BlockSpec((2, LANES, qk), lambda i, j: (0, 0, 0)),
                  pl.BlockSpec((2, 1, qk), lambda i, j: (0, 0, 0))],
        out_specs=[pl.BlockSpec((tm, tn), lambda i, j: (i, j)),
                   pl.BlockSpec((tm, tn), v_map),
                   pl.BlockSpec((tm, qk), lambda i, j: (i, 0)),
                   pl.BlockSpec((tm, qk), lambda i, j: (i, 0))],
        out_shape=[jax.ShapeDtypeStruct((rows, nj * tn), F32),
                   jax.ShapeDtypeStruct((rows, v_w), BF16),
                   jax.ShapeDtypeStruct((rows, qk), F32),
                   jax.ShapeDtypeStruct((rows, qk), F32)],
        scratch_shapes=[pltpu.VMEM((tm, d), BF16)],
        compiler_params=_cparams(("arbitrary", "arbitrary"), 48),
        name="proj_gla_pool",
    )(h, mod3, w_a, w_u, w_gate, gate_w2p, gate_b.reshape(2, 1, qk))


def _split3(x):
    x1 = x.astype(BF16)
    r1 = x - x1.astype(F32)
    x2 = r1.astype(BF16)
    x3 = (r1 - x2.astype(F32)).astype(BF16)
    return x1, x2, x3


def _gla_chunk(q_ref, k_ref, v_ref, g_ref, o_ref, s_ref, b_ref, base, head, reverse):
    C, SB = GLA_CHUNK, GLA_SUB
    nb = C // SB
    rows = pl.ds(base, C)
    kcols = slice(head * GLA_DK, (head + 1) * GLA_DK)
    vcols = slice(head * GLA_DV, (head + 1) * GLA_DV)
    q = q_ref[rows, kcols] * (GLA_DK ** -0.5)
    k = k_ref[rows, kcols]
    v = v_ref[rows, vcols]
    g = g_ref[rows, kcols]

    ti = lax.broadcasted_iota(jnp.int32, (C, C), 0)
    si = lax.broadcasted_iota(jnp.int32, (C, C), 1)
    tri = jnp.where((si >= ti) if reverse else (si <= ti), 1.0, 0.0).astype(BF16)
    b = sum(jnp.dot(tri, part, preferred_element_type=F32) for part in _split3(g))
    yield

    b_ref[0] = b
    b_ref[1] = k
    s_old = s_ref[...]
    inter = jnp.dot((q * jnp.exp(b)).astype(BF16), s_old.astype(BF16),
                    preferred_element_type=F32)
    last = 0 if reverse else C - 1
    b_last = b_ref[0, last:last + 1, :]
    ke = (k * jnp.exp(b_last - b)).astype(BF16)
    upd = lax.dot_general(ke, v, (((0,), (0,)), ((), ())), preferred_element_type=F32)
    raws = []
    for i in range(nb):
        blk = slice(i * SB, (i + 1) * SB)
        edge = (i + 1) * SB if reverse else i * SB - 1
        has_off = (i < nb - 1) if reverse else (i > 0)
        if has_off:
            r_i = b_ref[0, edge:edge + 1, :]
            qe = (q[blk] * jnp.exp(b[blk] - r_i)).astype(BF16)
            ke_i = (k * jnp.exp(jnp.minimum(r_i - b, 0.0))).astype(BF16)
            raws.append(lax.dot_general(qe, ke_i, (((1,), (1,)), ((), ())),
                                        preferred_element_type=F32))
        else:
            raws.append(None)
    yield

    trow = lax.broadcasted_iota(jnp.int32, (SB, LANES), 0)
    ones = jnp.ones((LANES, LANES), BF16)
    dsums = []
    for i in range(nb):
        blk = slice(i * SB, (i + 1) * SB)
        q_i = q[blk]
        b_i = b[blk]
        terms = []
        for s in range(SB):
            row = i * SB + s
            k_s = b_ref[1, row:row + 1, :]
            b_s = b_ref[0, row:row + 1, :]
            keep = (trow <= s) if reverse else (trow >= s)
            terms.append(jnp.where(keep, q_i * k_s * jnp.exp(b_i - b_s), 0.0))
        stacked = jnp.concatenate(terms, axis=0).astype(BF16)
        dsums.append(jnp.dot(stacked, ones, preferred_element_type=F32))
    yield

    col = lax.broadcasted_iota(jnp.int32, (SB, C), 1)
    lane = lax.broadcasted_iota(jnp.int32, (SB, LANES), 1)
    a_rows = []
    for i in range(nb):
        diag = jnp.zeros((SB, LANES), F32)
        for s in range(SB):
            diag = jnp.where(lane == i * SB + s, dsums[i][s * SB:(s + 1) * SB], diag)
        a_i = diag[:, :C]
        if raws[i] is not None:
            off_mask = (col >= (i + 1) * SB) if reverse else (col < i * SB)
            a_i = a_i + jnp.where(off_mask, raws[i], 0.0)
        a_rows.append(a_i)
    attn = jnp.concatenate(a_rows, axis=0).astype(BF16)
    intra = jnp.dot(attn, v, preferred_element_type=F32)
    decay = jnp.broadcast_to(jnp.exp(b_last), (GLA_DK, GLA_DK)).T
    s_ref[...] = jnp.concatenate([decay, decay], axis=1) * s_old + upd
    yield

    o_ref[rows, vcols] = inter + intra
    yield


GLA_STAGES = 5


def _gla_kernel(qf_ref, kf_ref, vf_ref, gf_ref, qb_ref, kb_ref, vb_ref, gb_ref,
                of_ref, ob_ref, *scratch):
    n_chain = 2 * GLA_HEADS
    state, stage = scratch[:n_chain], scratch[n_chain:]

    @pl.when(pl.program_id(0) == 0)
    def _():
        for s_ref in state:
            s_ref[...] = jnp.zeros_like(s_ref)

    n_sub = GLA_GROUP // GLA_CHUNK

    def body(c, carry):
        fbase = pl.multiple_of(c * GLA_CHUNK, GLA_CHUNK)
        rbase = pl.multiple_of((n_sub - 1 - c) * GLA_CHUNK, GLA_CHUNK)
        chains = []
        for head in range(GLA_HEADS):
            f, r = head, GLA_HEADS + head
            chains.append(_gla_chunk(qf_ref, kf_ref, vf_ref, gf_ref, of_ref, state[f], stage[f],
                                     fbase, head, False))
            chains.append(_gla_chunk(qb_ref, kb_ref, vb_ref, gb_ref, ob_ref, state[r], stage[r],
                                     rbase, head, True))
        for _ in range(GLA_STAGES):
            for chain in chains:
                next(chain)
        return carry

    lax.fori_loop(0, n_sub, body, 0)


def _gla(p0, v16, lf, lb, *, n_lat):
    rows = p0.shape[0]
    n_grp = rows // GLA_GROUP
    lat_grp = n_lat // GLA_GROUP
    G = GLA_GROUP
    qk_w = GLA_HEADS * GLA_DK
    v_w = GLA_HEADS * GLA_DV

    def fwd(c):
        return (c + lat_grp) % n_grp

    def bwd(c):
        return n_grp - 1 - c

    def spec(width, order, col):
        return pl.BlockSpec((G, width), lambda c: (order(c), col))

    in_specs = []
    for order in (fwd, bwd):
        in_specs += [spec(qk_w, order, 0), spec(qk_w, order, 1), spec(v_w, order, 0),
                     spec(qk_w, order, 0)]
    out_specs = [spec(v_w, fwd, 0), spec(v_w, bwd, 0)]
    return pl.pallas_call(
        _gla_kernel,
        grid=(n_grp,),
        in_specs=in_specs,
        out_specs=out_specs,
        out_shape=[jax.ShapeDtypeStruct((rows, v_w), F32)] * 2,
        scratch_shapes=([pltpu.VMEM((GLA_DK, GLA_DV), F32)] * (2 * GLA_HEADS)
                        + [pltpu.VMEM((2, GLA_CHUNK, GLA_DK), F32)] * (2 * GLA_HEADS)),
        compiler_params=_cparams(("arbitrary",), 32),
        name="gla_scan",
    )(p0, p0, v16, lf, p0, p0, v16, lb)


def _mix0_out_kernel(h_ref, mod_ref, of_ref, ob_ref, r_ref, uprev_ref, u_ref, unext_ref,
                     nw_ref, pw_ref, ps_ref, wout_ref, o_ref, ext_ref,
                     *, n_lat, n_ctx, tm):
    i = pl.program_id(0)
    lat_tiles = n_lat // tm
    n_tiles = (n_lat + n_ctx) // tm
    in_ctx = i >= lat_tiles
    is_first = (i == 0) | (i == lat_tiles)
    is_last = (i == lat_tiles - 1) | (i == n_tiles - 1)

    o = of_ref[...] + ob_ref[...]
    r = r_ref[...]
    heads = []
    for hd in range(GLA_HEADS):
        heads.append(_rms(o[:, hd * GLA_DV:(hd + 1) * GLA_DV]) * nw_ref[...])
    gl = jnp.concatenate(heads, axis=1) * (r * jax.nn.sigmoid(r))

    u = u_ref[...]
    H = POOL_HALO
    ext_ref[0:H, :] = jnp.where(is_first, 0.0, uprev_ref[...])
    ext_ref[H:H + tm, :] = u
    ext_ref[H + tm:2 * H + tm, :] = jnp.where(is_last, 0.0, unext_ref[...])
    seq_len = jnp.where(in_ctx, n_ctx, n_lat)
    t = (i - jnp.where(in_ctx, lat_tiles, 0)) * tm + lax.broadcasted_iota(jnp.int32, (tm, 1), 0)
    pooled = []
    for gi, w in enumerate(POOL_WINDOWS):
        cols = slice(gi * POOL_GC, (gi + 1) * POOL_GC)
        s = ext_ref[H - w // 2:H - w // 2 + tm, cols]
        for dlt in range(-w // 2 + 1, w - w // 2):
            s = s + ext_ref[H + dlt:H + dlt + tm, cols]
        lo = jnp.maximum(t - w // 2, 0)
        hi = jnp.minimum(t + (w - w // 2), seq_len)
        cnt = (hi - lo).astype(F32)
        pg = (s / cnt - u[:, cols]).astype(BF16)
        pooled.append(jnp.dot(pg, pw_ref[gi], preferred_element_type=F32) * ps_ref[:, cols])

    mix = jnp.concatenate([gl] + pooled, axis=1).astype(BF16)
    y = jnp.dot(mix, wout_ref[...], preferred_element_type=F32)
    is_ctx = _is_ctx_rows(i * tm, tm, n_lat)
    o_ref[...] = h_ref[...] + _mod_row(mod_ref, is_ctx, 2) * y


def _mix0_out(h, mod3, o_f, o_b, p0, norm_w, pool_w, pool_scale, w_out, *, n_lat, n_ctx):
    rows, d = h.shape
    tm = 256
    gw = GLA_HEADS * GLA_DV
    pwid = len(POOL_WINDOWS) * POOL_GC
    hb = tm // POOL_HALO
    n_hblk = rows // POOL_HALO
    r_col = (2 * GLA_HEADS * GLA_DK + gw) // gw
    u_col = (2 * GLA_HEADS * GLA_DK + 2 * gw) // pwid
    return pl.pallas_call(
        functools.partial(_mix0_out_kernel, n_lat=n_lat, n_ctx=n_ctx, tm=tm),
        grid=(rows // tm,),
        in_specs=[pl.BlockSpec((tm, d), lambda i: (i, 0)),
                  pl.BlockSpec((2, 3, d), lambda i: (0, 0, 0)),
                  pl.BlockSpec((tm, gw), lambda i: (i, 0)),
                  pl.BlockSpec((tm, gw), lambda i: (i, 0)),
                  pl.BlockSpec((tm, gw), lambda i: (i, r_col)),
                  pl.BlockSpec((POOL_HALO, pwid), lambda i: (jnp.maximum(i * hb - 1, 0), u_col)),
                  pl.BlockSpec((tm, pwid), lambda i: (i, u_col)),
                  pl.BlockSpec((POOL_HALO, pwid),
                               lambda i: (jnp.minimum((i + 1) * hb, n_hblk - 1), u_col)),
                  pl.BlockSpec((1, GLA_DV), lambda i: (0, 0)),
                  pl.BlockSpec((len(POOL_WINDOWS), POOL_GC, POOL_GC), lambda i: (0, 0, 0)),
                  pl.BlockSpec((1, pwid), lambda i: (0, 0)),
                  pl.BlockSpec((d, d), lambda i: (0, 0))],
        out_specs=pl.BlockSpec((tm, d), lambda i: (i, 0)),
        out_shape=jax.ShapeDtypeStruct((rows, d), F32),
        scratch_shapes=[pltpu.VMEM((tm + 2 * POOL_HALO, pwid), F32)],
        compiler_params=_cparams(("arbitrary",), 48),
        name="mix0_readout",
    )(h, mod3, o_f, o_b, p0, p0, p0, p0, norm_w.reshape(1, GLA_DV), pool_w,
      pool_scale.reshape(1, pwid), w_out)


def _proj1_kernel(h_ref, mod_ref, w_ref, cos_ref, sin_ref, qk_ref, vt_ref, z_ref,
                  *, n_lat, tm, tn, n_qt, qscale):
    i = pl.program_id(0)
    j = pl.program_id(1)

    @pl.when(j == 0)
    def _():
        _store_modulated(h_ref, mod_ref, z_ref, i * tm, tm, n_lat)

    @pl.when(j < 2 * n_qt)
    def _():
        acc = jnp.dot(z_ref[...], w_ref[...], preferred_element_type=F32)
        scale = jnp.where(j < n_qt, qscale, 1.0)
        cs = cos_ref[...] * scale
        sn = sin_ref[...] * scale
        for gidx in range(tn // LANES):
            x = acc[:, gidx * LANES:(gidx + 1) * LANES]
            y = x * cs + pltpu.roll(x, LANES // 2, 1) * sn
            qk_ref[:, gidx * LANES:(gidx + 1) * LANES] = y.astype(BF16)

    @pl.when(j >= 2 * n_qt)
    def _():
        vt = jnp.dot(z_ref[...], w_ref[...], preferred_element_type=F32).T
        for hd in range(tn // DIFF_DV):
            vt_ref[hd] = vt[hd * DIFF_DV:(hd + 1) * DIFF_DV].astype(BF16)


def _proj1(h, mod3, w_qkv, cos_t, sin_t, *, tm, n_lat):
    rows, d = h.shape
    tn = 1024
    n_qt = d // tn
    n_vt = w_qkv.shape[1] // tn - 2 * n_qt
    hpt = tn // DIFF_DV
    qscale = (DIFF_DH ** -0.5) * math.log2(math.e)
    return pl.pallas_call(
        functools.partial(_proj1_kernel, n_lat=n_lat, tm=tm, tn=tn, n_qt=n_qt, qscale=qscale),
        grid=(rows // tm, 2 * n_qt + n_vt),
        in_specs=[pl.BlockSpec((tm, d), lambda i, j: (i, 0)),
                  pl.BlockSpec((2, 3, d), lambda i, j: (0, 0, 0)),
                  pl.BlockSpec((d, tn), lambda i, j: (0, j)),
                  pl.BlockSpec((tm, LANES), lambda i, j: (i, 0)),
                  pl.BlockSpec((tm, LANES), lambda i, j: (i, 0))],
        out_specs=[pl.BlockSpec((tm, tn), lambda i, j: (i, jnp.minimum(j, 2 * n_qt - 1))),
                   pl.BlockSpec((hpt, None, DIFF_DV, tm),
                                lambda i, j: (jnp.maximum(j - 2 * n_qt, 0), i, 0, 0))],
        out_shape=[jax.ShapeDtypeStruct((rows, 2 * d), BF16),
                   jax.ShapeDtypeStruct((DIFF_HEADS, rows // tm, DIFF_DV, tm), BF16)],
        scratch_shapes=[pltpu.VMEM((tm, d), BF16)],
        compiler_params=_cparams(("arbitrary", "arbitrary"), 52),
        name="proj_qkv_rope",
    )(h, mod3, w_qkv, cos_t, sin_t)


ATT_TILE = 256


def _attn_kernel(lam_ref, q_ref, k_ref, vt_ref, nw_ref, o_ref, m_ref, l_ref, acc_ref,
                 sa_ref, sb_ref, ca_ref, cb_ref, p_ref, *, tk, n_kv, lam_init):
    tq = q_ref.shape[0]
    T = ATT_TILE
    SUB = 8
    m_ref[...] = jnp.full_like(m_ref, -jnp.inf)
    l_ref[...] = jnp.zeros_like(l_ref)
    acc_ref[...] = jnp.zeros_like(acc_ref)

    units = [(c, qb) for qb in range(tq // T) for c in range(2)]

    def scores(jj, s_ref, c_ref, c, qb):
        comp = slice(c * DIFF_DH, (c + 1) * DIFF_DH)
        cols = slice(qb * T, (qb + 1) * T)
        qc = q_ref[cols, comp]
        cmax = None
        for kb in range(tk // T):
            kc = k_ref[jj * tk + kb * T:jj * tk + (kb + 1) * T, comp]
            st = lax.dot_general(kc, qc, (((1,), (1,)), ((), ())), preferred_element_type=F32)
            s_ref[c, kb * T:(kb + 1) * T, cols] = st
            bmax = jnp.max(st.reshape(T // SUB, SUB, T), axis=0)
            cmax = bmax if cmax is None else jnp.maximum(cmax, bmax)
        c_ref[c, :, cols] = cmax

    def accumulate(jj, s_ref, c_ref, c, qb):
        cols = slice(qb * T, (qb + 1) * T)
        m_old = m_ref[c, :, cols]
        m_new = jnp.maximum(m_old, jnp.max(c_ref[c, :, cols], axis=0, keepdims=True))
        alpha = jnp.exp2(m_old - m_new)
        lsum = jnp.zeros((SUB, T), F32)
        for kb in range(tk // T):
            keys = slice(kb * T, (kb + 1) * T)
            pt = jnp.exp2(s_ref[c, keys, cols] - m_new)
            lsum = lsum + jnp.sum(pt.reshape(T // SUB, SUB, T), axis=0)
            p_ref[c, keys, cols] = pt.astype(BF16)
        pv = jnp.dot(vt_ref[jj], p_ref[c, :, cols], preferred_element_type=F32)
        l_ref[c, :, cols] = alpha * l_ref[c, :, cols] + lsum
        acc_ref[c, :, cols] = alpha * acc_ref[c, :, cols] + pv
        m_ref[c, :, cols] = m_new

    slots = ((sa_ref, ca_ref), (sb_ref, cb_ref))
    for c, qb in units:
        scores(0, *slots[0], c, qb)
    for jj in range(n_kv):
        for c, qb in units:
            accumulate(jj, *slots[jj % 2], c, qb)
            if jj + 1 < n_kv:
                scores(jj + 1, *slots[(jj + 1) % 2], c, qb)

    lp = lam_ref[...]
    lam = (jnp.exp(jnp.sum(lp[0:1] * lp[1:2], axis=-1, keepdims=True))
           - jnp.exp(jnp.sum(lp[2:3] * lp[3:4], axis=-1, keepdims=True)) + lam_init)
    l0 = jnp.sum(l_ref[0], axis=0, keepdims=True)
    l1 = jnp.sum(l_ref[1], axis=0, keepdims=True)
    ot = acc_ref[0] / l0 - lam * (acc_ref[1] / l1)
    ms = jnp.mean(ot * ot, axis=0, keepdims=True)
    ont = ot * lax.rsqrt(ms + NORM_EPS) * (nw_ref[...] * (1.0 - lam_init))
    o_ref[...] = ont.T.astype(BF16)


def _diff_attn(qk, vt, lam_p, norm_w, *, n_lat, lam_init):
    rows = qk.shape[0]
    d = DIFF_HEADS * DIFF_DV
    n_kv, tk = vt.shape[1], vt.shape[3]
    tq = 512
    return pl.pallas_call(
        functools.partial(_attn_kernel, tk=tk, n_kv=n_kv, lam_init=lam_init),
        grid=(DIFF_HEADS, n_lat // tq),
        in_specs=[pl.BlockSpec((4, DIFF_DH), lambda h, i: (0, 0)),
                  pl.BlockSpec((tq, DIFF_DV), lambda h, i: (i, h)),
                  pl.BlockSpec((rows, DIFF_DV), lambda h, i: (0, DIFF_HEADS + h)),
                  pl.BlockSpec((None, n_kv, DIFF_DV, tk), lambda h, i: (h, 0, 0, 0)),
                  pl.BlockSpec((DIFF_DV, 1), lambda h, i: (0, 0))],
        out_specs=pl.BlockSpec((tq, DIFF_DV), lambda h, i: (i, h)),
        out_shape=jax.ShapeDtypeStruct((n_lat, d), BF16),
        scratch_shapes=[pltpu.VMEM((2, 1, tq), F32), pltpu.VMEM((2, 8, tq), F32),
                        pltpu.VMEM((2, DIFF_DV, tq), F32),
                        pltpu.VMEM((2, tk, tq), F32), pltpu.VMEM((2, tk, tq), F32),
                        pltpu.VMEM((2, 8, tq), F32), pltpu.VMEM((2, 8, tq), F32),
                        pltpu.VMEM((2, tk, tq), BF16)],
        compiler_params=_cparams(("arbitrary", "arbitrary"), 48),
        name="diff_attn",
    )(lam_p, qk, qk, vt, norm_w.reshape(DIFF_DV, 1))


def _outproj_kernel(h_ref, mod_ref, x_ref, w_ref, o_ref):
    y = jnp.dot(x_ref[...], w_ref[...], preferred_element_type=F32)
    o_ref[...] = h_ref[...] + mod_ref[0, 2:3, :] * y


def _outproj_latent(h, mod3, x, w, *, n_lat):
    d = h.shape[1]
    tm = 512
    return pl.pallas_call(
        _outproj_kernel,
        grid=(n_lat // tm,),
        in_specs=[pl.BlockSpec((tm, d), lambda i: (i, 0)),
                  pl.BlockSpec((2, 3, d), lambda i: (0, 0, 0)),
                  pl.BlockSpec((tm, d), lambda i: (i, 0)),
                  pl.BlockSpec((d, d), lambda i: (0, 0))],
        out_specs=pl.BlockSpec((tm, d), lambda i: (i, 0)),
        out_shape=jax.ShapeDtypeStruct((n_lat, d), F32),
        compiler_params=_cparams(("arbitrary",), 40),
        name="attn_outproj",
    )(h, mod3, x, w)


def _rope_tables(n_lat, n_ctx):
    rows = n_lat // GRID_W
    row = jnp.repeat(jnp.arange(rows), GRID_W).astype(F32)
    col = jnp.tile(jnp.arange(GRID_W), rows).astype(F32)
    n_freq = DIFF_DH // 4
    inv_freq = ROPE_THETA ** (-jnp.arange(n_freq, dtype=F32) / n_freq)
    ang = jnp.concatenate([row[:, None] * inv_freq, col[:, None] * inv_freq], axis=-1)
    cos, sin = jnp.cos(ang), jnp.sin(ang)
    cos_t = jnp.concatenate([cos, cos], axis=-1)
    sin_t = jnp.concatenate([-sin, sin], axis=-1)
    cos_t = jnp.concatenate([cos_t, jnp.ones((n_ctx, DIFF_DH), F32)], axis=0)
    sin_t = jnp.concatenate([sin_t, jnp.zeros((n_ctx, DIFF_DH), F32)], axis=0)
    return cos_t, sin_t


def kernel(x, c, ctx, c_ctx, ada_w, ada_b, ffn_w1, ffn_w3, ffn_w2, gla_w_in, gla_gate_w2,
           gla_gate_b, gla_norm_w, pool_w, pool_scale, mix0_w_out, diff_w_qkv, diff_lambda,
           diff_norm_w, diff_w_out, final_norm_w):
    assert x.shape[0] == 1 and ada_w.shape[0] == 2
    n_lat, d = x.shape[1], x.shape[2]
    n_ctx = ctx.shape[1]
    rows = n_lat + n_ctx
    tm = 768
    assert rows % tm == 0 and n_lat % 512 == 0 and n_ctx % 256 == 0 and n_lat % GRID_W == 0

    h = jnp.concatenate([x[0], ctx[0]], axis=0)
    c8 = jnp.zeros((8, d), F32).at[0].set(c[0]).at[1].set(c_ctx)
    mods = _ada(c8, ada_w, ada_b)[:, :2].reshape(2, 2, 9, d)

    def first_weights(layer, half):
        return tuple(wt[layer, half].astype(BF16) for wt in (ffn_w1, ffn_w3, ffn_w2))

    def following(layer, half):
        return (ffn_w1, ffn_w3, ffn_w2, layer, half)

    m = mods[0]
    h, w_next = _ffn(h, m[:, 0:3], first_weights(0, 0), rows=rows, tm=tm, n_lat=n_lat,
                     nxt=following(0, 1))
    w_in = gla_w_in[0]
    n_qkvr = 2 * GLA_HEADS * GLA_DK + 2 * GLA_HEADS * GLA_DV
    n_gate = 2 * GLA_GATE_RANK
    w_a = w_in[:, :n_qkvr].astype(BF16)
    w_u = w_in[:, n_qkvr + n_gate:].astype(BF16)
    w_gate = jnp.pad(w_in[:, n_qkvr:n_qkvr + n_gate], ((0, 0), (0, LANES - n_gate))).astype(BF16)
    gw2 = gla_gate_w2[0].astype(BF16)
    gw2p = jnp.zeros((2, LANES, gw2.shape[-1]), BF16)
    gw2p = gw2p.at[0, :GLA_GATE_RANK].set(gw2[0]).at[1, GLA_GATE_RANK:n_gate].set(gw2[1])
    p0, v16, lf, lb = _proj0(h, m[:, 3:6], w_a, w_u, w_gate, gw2p, gla_gate_b[0], tm=tm,
                             n_lat=n_lat)
    o_f, o_b = _gla(p0, v16, lf, lb, n_lat=n_lat)
    h = _mix0_out(h, m[:, 3:6], o_f, o_b, p0, gla_norm_w[0], pool_w[0].astype(BF16),
                  pool_scale[0], mix0_w_out[0].astype(BF16), n_lat=n_lat, n_ctx=n_ctx)
    h, w_next = _ffn(h, m[:, 6:9], w_next, rows=rows, tm=tm, n_lat=n_lat, nxt=following(1, 0))

    m = mods[1]
    h, w_next = _ffn(h, m[:, 0:3], w_next, rows=rows, tm=tm, n_lat=n_lat, nxt=following(1, 1))
    cos_t, sin_t = _rope_tables(n_lat, n_ctx)
    qk, vt = _proj1(h, m[:, 3:6], diff_w_qkv[0].astype(BF16), cos_t, sin_t, tm=tm, n_lat=n_lat)
    lam_init = 0.8 - 0.6 * math.exp(-0.3 * 1)
    o = _diff_attn(qk, vt, diff_lambda[0], diff_norm_w[0], n_lat=n_lat, lam_init=lam_init)
    hl = _outproj_latent(h, m[:, 3:6], o, diff_w_out[0].astype(BF16), n_lat=n_lat)
    out = _ffn(hl, m[:, 6:9], w_next, rows=n_lat, tm=512, n_lat=n_lat, final_w=final_norm_w)
    return out[None]
```

```python
import functools
import math

import jax
import jax.numpy as jnp
from jax import lax
from jax.experimental import pallas as pl
from jax.experimental.pallas import tpu as pltpu

F32 = jnp.float32
BF16 = jnp.bfloat16

NORM_EPS = 1e-6
GRID_W = 64
ROPE_THETA = 10000.0

GLA_HEADS = 4
GLA_DK = 128
GLA_DV = 256
GLA_GATE_RANK = 16
GLA_TAU = 16.0
GLA_CHUNK = 64
GLA_SUB = 16
GLA_GROUP = 256
POOL_WINDOWS = (2, 4, 8, 16)
POOL_GC = 256
POOL_HALO = 8

DIFF_HEADS = 8
DIFF_DH = 128
DIFF_DV = 256

LANES = 128
MIB = 1024 * 1024


def _cparams(semantics, vmem_mib):
    return pltpu.CompilerParams(dimension_semantics=semantics,
                                vmem_limit_bytes=int(vmem_mib * MIB))


def _rms(x):
    return x * lax.rsqrt(jnp.mean(x * x, axis=-1, keepdims=True) + NORM_EPS)


def _is_ctx_rows(row0, rows, n_lat):
    return (row0 + lax.broadcasted_iota(jnp.int32, (rows, 1), 0)) >= n_lat


def _mod_row(mod_ref, is_ctx, k):
    return jnp.where(is_ctx, mod_ref[1, k:k + 1, :], mod_ref[0, k:k + 1, :])


def _modulated(h, mod_ref, is_ctx):
    return _rms(h) * (1.0 + _mod_row(mod_ref, is_ctx, 1)) + _mod_row(mod_ref, is_ctx, 0)


NORM_ROWS = 32


def _store_modulated(h_ref, mod_ref, z_ref, row0, tm, n_lat):
    def body(r, carry):
        start = pl.multiple_of(r * NORM_ROWS, NORM_ROWS)
        rows = pl.ds(start, NORM_ROWS)
        which = jnp.where(row0 + start >= n_lat, 1, 0)
        shift = mod_ref[which, 0:1, :]
        scale = mod_ref[which, 1:2, :]
        z_ref[rows, :] = (_rms(h_ref[rows, :]) * (1.0 + scale) + shift).astype(BF16)
        return carry

    lax.fori_loop(0, tm // NORM_ROWS, body, 0, unroll=4)


def _ada_kernel(c_ref, w_ref, b_ref, o_ref):
    c = c_ref[...]
    s = (c * jax.nn.sigmoid(c)).astype(BF16)
    o_ref[0] = jnp.dot(s, w_ref[0].astype(BF16), preferred_element_type=F32) + b_ref[0]


def _ada(c8, ada_w, ada_b):
    depth, d, n = ada_w.shape
    tn = 1024
    return pl.pallas_call(
        _ada_kernel,
        grid=(depth, n // tn),
        in_specs=[pl.BlockSpec((8, d), lambda i, j: (0, 0)),
                  pl.BlockSpec((1, d, tn), lambda i, j: (i, 0, j)),
                  pl.BlockSpec((1, 1, tn), lambda i, j: (i, 0, j))],
        out_specs=pl.BlockSpec((1, 8, tn), lambda i, j: (i, 0, j)),
        out_shape=jax.ShapeDtypeStruct((depth, 8, n), F32),
        compiler_params=_cparams(("arbitrary", "arbitrary"), 40),
        name="ada_mod",
    )(c8, ada_w, ada_b.reshape(depth, 1, n))


def _ffn_kernel(*refs, n_lat, tm, final, starts):
    if final:
        h_ref, mod_ref, w1_ref, w3_ref, w2_ref, fnw_ref, o_ref, z_ref = refs
        casts = ()
    else:
        (h_ref, mod_ref, w1_ref, w3_ref, w2_ref, n1_ref, n3_ref, n2_ref,
         o_ref, c1_ref, c3_ref, c2_ref, z_ref) = refs
        casts = ((n1_ref, c1_ref), (n3_ref, c3_ref), (n2_ref, c2_ref))
    i = pl.program_id(0)
    j = pl.program_id(1)
    is_ctx = _is_ctx_rows(i * tm, tm, n_lat)

    @pl.when(j == 0)
    def _():
        _store_modulated(h_ref, mod_ref, z_ref, i * tm, tm, n_lat)
        o_ref[...] = jnp.zeros_like(o_ref)

    step = i * pl.num_programs(1) + j
    for m, (src_ref, dst_ref) in enumerate(casts):
        @pl.when((step >= starts[m]) & (step < starts[m + 1]))
        def _(src_ref=src_ref, dst_ref=dst_ref):
            dst_ref[...] = src_ref[...].astype(BF16)

    z = z_ref[...]
    u = jnp.dot(z, w1_ref[...], preferred_element_type=F32)
    g = jnp.dot(z, w3_ref[...], preferred_element_type=F32)
    a = (u * jax.nn.sigmoid(u) * g).astype(BF16)
    o_ref[...] += jnp.dot(a, w2_ref[...], preferred_element_type=F32)

    @pl.when(j == pl.num_programs(1) - 1)
    def _():
        hn = h_ref[...] + 0.5 * _mod_row(mod_ref, is_ctx, 2) * o_ref[...]
        if final:
            hn = _rms(hn) * fnw_ref[...]
        o_ref[...] = hn


def _ffn(h, mod3, w, *, rows, tm, n_lat, nxt=None, final_w=None):
    w1, w3, w2 = w
    d = h.shape[1]
    dff = w1.shape[-1]
    tf = 512
    nj = dff // tf
    final = final_w is not None
    in_specs = [pl.BlockSpec((tm, d), lambda i, j: (i, 0)),
                pl.BlockSpec((2, 3, d), lambda i, j: (0, 0, 0)),
                pl.BlockSpec((d, tf), lambda i, j: (0, j)),
                pl.BlockSpec((d, tf), lambda i, j: (0, j)),
                pl.BlockSpec((tf, d), lambda i, j: (j, 0))]
    args = [h, mod3, w1, w3, w2]
    out_specs = [pl.BlockSpec((tm, d), lambda i, j: (i, 0))]
    out_shape = [jax.ShapeDtypeStruct((rows, d), F32)]
    starts = ()
    if final:
        in_specs.append(pl.BlockSpec((1, d), lambda i, j: (0, 0)))
        args.append(final_w.reshape(1, d))
    else:
        f1, f3, f2, layer, half = nxt
        cc = tf
        per_row = dff // cc
        steps = (rows // tm) * nj
        cr_up, cr_down = next((a, b) for a, b in ((d // 4, d // 2), (d // 2, d // 2), (d, d))
                              if (2 * (d // a) + d // b) * per_row <= steps)
        counts = ((d // cr_up) * per_row, (d // cr_up) * per_row, (d // cr_down) * per_row)
        starts = (0, counts[0], counts[0] + counts[1], sum(counts))

        def blk(i, j, m):
            return jnp.clip(i * nj + j - starts[m], 0, counts[m] - 1)

        def up_map(m):
            return lambda i, j: (blk(i, j, m) // per_row, blk(i, j, m) % per_row)

        def down_map(m):
            return lambda i, j: (blk(i, j, m) % per_row, blk(i, j, m) // per_row)

        def stacked(index_map):
            return lambda i, j: (layer, half) + index_map(i, j)

        in_specs += [pl.BlockSpec((None, None, cr_up, cc), stacked(up_map(0))),
                     pl.BlockSpec((None, None, cr_up, cc), stacked(up_map(1))),
                     pl.BlockSpec((None, None, cc, cr_down), stacked(down_map(2)))]
        args += [f1, f3, f2]
        out_specs += [pl.BlockSpec((cr_up, cc), up_map(0)), pl.BlockSpec((cr_up, cc), up_map(1)),
                      pl.BlockSpec((cc, cr_down), down_map(2))]
        out_shape += [jax.ShapeDtypeStruct((d, dff), BF16), jax.ShapeDtypeStruct((d, dff), BF16),
                      jax.ShapeDtypeStruct((dff, d), BF16)]
    res = pl.pallas_call(
        functools.partial(_ffn_kernel, n_lat=n_lat, tm=tm, final=final, starts=starts),
        grid=(rows // tm, nj),
        in_specs=in_specs,
        out_specs=out_specs,
        out_shape=out_shape,
        scratch_shapes=[pltpu.VMEM((tm, d), BF16)],
        compiler_params=_cparams(("arbitrary", "arbitrary"), 56 if final else 60),
        name="ffn_final" if final else "ffn_half",
    )(*args)
    return res[0] if final else (res[0], tuple(res[1:]))


def _log_sigmoid(x):
    return jnp.minimum(x, 0.0) - jnp.log1p(jnp.exp(-jnp.abs(x)))


def _proj0_kernel(h_ref, mod_ref, wa_ref, wu_ref, wg_ref, gw2_ref, gb_ref,
                  p_ref, v16_ref, lf_ref, lb_ref, z_ref, *, n_lat, tm, n_a, v_tiles):
    i = pl.program_id(0)
    j = pl.program_id(1)

    is_v = (j >= v_tiles[0]) & (j < v_tiles[1])

    @pl.when(j == 0)
    def _():
        _store_modulated(h_ref, mod_ref, z_ref, i * tm, tm, n_lat)
        z = z_ref[...]
        p_ref[...] = jnp.dot(z, wa_ref[...], preferred_element_type=F32)
        gz = jnp.dot(z, wg_ref[...], preferred_element_type=F32).astype(BF16)
        for d, out in ((0, lf_ref), (1, lb_ref)):
            zz = jnp.dot(gz, gw2_ref[d], preferred_element_type=F32) + gb_ref[d]
            out[...] = _log_sigmoid(zz) / GLA_TAU

    @pl.when((j > 0) & (j < n_a) & jnp.logical_not(is_v))
    def _():
        p_ref[...] = jnp.dot(z_ref[...], wa_ref[...], preferred_element_type=F32)

    @pl.when(is_v)
    def _():
        acc = jnp.dot(z_ref[...], wa_ref[...], preferred_element_type=F32)
        p_ref[...] = acc
        v16_ref[...] = acc.astype(BF16)

    @pl.when(j >= n_a)
    def _():
        p_ref[...] = jnp.dot(z_ref[...], wu_ref[...], preferred_element_type=F32)


def _proj0(h, mod3, w_a, w_u, w_gate, gate_w2p, gate_b, *, tm, n_lat):
    rows, d = h.shape
    qk = gate_b.shape[-1]
    tn = 1024
    n_a, n_u = w_a.shape[1] // tn, w_u.shape[1] // tn
    nj = n_a + n_u
    v_lo, v_w = 2 * qk, GLA_HEADS * GLA_DV
    v_tiles = (v_lo // tn, (v_lo + v_w) // tn)

    def v_map(i, j):
        return (i, jnp.clip(j - v_tiles[0], 0, v_tiles[1] - v_tiles[0] - 1))

    return pl.pallas_call(
        functools.partial(_proj0_kernel, n_lat=n_lat, tm=tm, n_a=n_a, v_tiles=v_tiles),
        grid=(rows // tm, nj),
        in_specs=[pl.BlockSpec((tm, d), lambda i, j: (i, 0)),
                  pl.BlockSpec((2, 3, d), lambda i, j: (0, 0, 0)),
                  pl.BlockSpec((d, tn), lambda i, j: (0, jnp.minimum(j, n_a - 1))),
                  pl.BlockSpec((d, tn), lambda i, j: (0, jnp.maximum(j - n_a, 0))),
                  pl.BlockSpec((d, LANES), lambda i, j: (0, 0)),
                  pl.BlockSpec((2, LANES, qk), lambda i, j: (0, 0, 0)),
                  pl.BlockSpec((2, 1, qk), lambda i, j: (0, 0, 0))],
        out_specs=[pl.BlockSpec((tm, tn), lambda i, j: (i, j)),
                   pl.BlockSpec((tm, tn), v_map),
                   pl.BlockSpec((tm, qk), lambda i, j: (i, 0)),
                   pl.BlockSpec((tm, qk), lambda i, j: (i, 0))],
        out_shape=[jax.ShapeDtypeStruct((rows, nj * tn), F32),
                   jax.ShapeDtypeStruct((rows, v_w), BF16),
                   jax.ShapeDtypeStruct((rows, qk), F32),
                   jax.ShapeDtypeStruct((rows, qk), F32)],
        scratch_shapes=[pltpu.VMEM((tm, d), BF16)],
        compiler_params=_cparams(("arbitrary", "arbitrary"), 56),
        name="proj_gla_pool",
    )(h, mod3, w_a, w_u, w_gate, gate_w2p, gate_b.reshape(2, 1, qk))


def _split3(x):
    x1 = x.astype(BF16)
    r1 = x - x1.astype(F32)
    x2 = r1.astype(BF16)
    x3 = (r1 - x2.astype(F32)).astype(BF16)
    return x1, x2, x3


def _gla_chunk(q_ref, k_ref, v_ref, g_ref, o_ref, s_ref, b_ref, base, head, reverse):
    C, SB = GLA_CHUNK, GLA_SUB
    nb = C // SB
    rows = pl.ds(base, C)
    kcols = slice(head * GLA_DK, (head + 1) * GLA_DK)
    vcols = slice(head * GLA_DV, (head + 1) * GLA_DV)
    q = q_ref[rows, kcols] * (GLA_DK ** -0.5)
    k = k_ref[rows, kcols]
    v = v_ref[rows, vcols]
    g = g_ref[rows, kcols]

    ti = lax.broadcasted_iota(jnp.int32, (C, C), 0)
    si = lax.broadcasted_iota(jnp.int32, (C, C), 1)
    tri = jnp.where((si >= ti) if reverse else (si <= ti), 1.0, 0.0).astype(BF16)
    b = sum(jnp.dot(tri, part, preferred_element_type=F32) for part in _split3(g))
    yield

    b_ref[0] = b
    b_ref[1] = k
    s_old = s_ref[...]
    inter = jnp.dot((q * jnp.exp(b)).astype(BF16), s_old.astype(BF16),
                    preferred_element_type=F32)
    last = 0 if reverse else C - 1
    b_last = b_ref[0, last:last + 1, :]
    ke = (k * jnp.exp(b_last - b)).astype(BF16)
    upd = lax.dot_general(ke, v, (((0,), (0,)), ((), ())), preferred_element_type=F32)
    raws = []
    for i in range(nb):
        blk = slice(i * SB, (i + 1) * SB)
        edge = (i + 1) * SB if reverse else i * SB - 1
        has_off = (i < nb - 1) if reverse else (i > 0)
        if has_off:
            r_i = b_ref[0, edge:edge + 1, :]
            qe = (q[blk] * jnp.exp(b[blk] - r_i)).astype(BF16)
            ke_i = (k * jnp.exp(jnp.minimum(r_i - b, 0.0))).astype(BF16)
            raws.append(lax.dot_general(qe, ke_i, (((1,), (1,)), ((), ())),
                                        preferred_element_type=F32))
        else:
            raws.append(None)
    yield

    trow = lax.broadcasted_iota(jnp.int32, (SB, LANES), 0)
    ones = jnp.ones((LANES, LANES), BF16)
    dsums = []
    for i in range(nb):
        blk = slice(i * SB, (i + 1) * SB)
        q_i = q[blk]
        b_i = b[blk]
        terms = []
        for s in range(SB):
            row = i * SB + s
            k_s = b_ref[1, row:row + 1, :]
            b_s = b_ref[0, row:row + 1, :]
            keep = (trow <= s) if reverse else (trow >= s)
            terms.append(jnp.where(keep, q_i * k_s * jnp.exp(b_i - b_s), 0.0))
        stacked = jnp.concatenate(terms, axis=0).astype(BF16)
        dsums.append(jnp.dot(stacked, ones, preferred_element_type=F32))
    yield

    col = lax.broadcasted_iota(jnp.int32, (SB, C), 1)
    lane = lax.broadcasted_iota(jnp.int32, (SB, LANES), 1)
    a_rows = []
    for i in range(nb):
        diag = jnp.zeros((SB, LANES), F32)
        for s in range(SB):
            diag = jnp.where(lane == i * SB + s, dsums[i][s * SB:(s + 1) * SB], diag)
        a_i = diag[:, :C]
        if raws[i] is not None:
            off_mask = (col >= (i + 1) * SB) if reverse else (col < i * SB)
            a_i = a_i + jnp.where(off_mask, raws[i], 0.0)
        a_rows.append(a_i)
    attn = jnp.concatenate(a_rows, axis=0).astype(BF16)
    intra = jnp.dot(attn, v, preferred_element_type=F32)
    decay = jnp.broadcast_to(jnp.exp(b_last), (GLA_DK, GLA_DK)).T
    s_ref[...] = jnp.concatenate([decay, decay], axis=1) * s_old + upd
    yield

    o_ref[rows, vcols] = inter + intra
    yield


GLA_STAGES = 5


def _gla_kernel(qf_ref, kf_ref, vf_ref, gf_ref, qb_ref, kb_ref, vb_ref, gb_ref,
                of_ref, ob_ref, *scratch):
    n_chain = 2 * GLA_HEADS
    state, stage = scratch[:n_chain], scratch[n_chain:]

    @pl.when(pl.program_id(0) == 0)
    def _():
        for s_ref in state:
            s_ref[...] = jnp.zeros_like(s_ref)

    n_sub = GLA_GROUP // GLA_CHUNK

    def body(c, carry):
        fbase = pl.multiple_of(c * GLA_CHUNK, GLA_CHUNK)
        rbase = pl.multiple_of((n_sub - 1 - c) * GLA_CHUNK, GLA_CHUNK)
        chains = []
        for head in range(GLA_HEADS):
            f, r = head, GLA_HEADS + head
            chains.append(_gla_chunk(qf_ref, kf_ref, vf_ref, gf_ref, of_ref, state[f], stage[f],
                                     fbase, head, False))
            chains.append(_gla_chunk(qb_ref, kb_ref, vb_ref, gb_ref, ob_ref, state[r], stage[r],
                                     rbase, head, True))
        for _ in range(GLA_STAGES):
            for chain in chains:
                next(chain)
        return carry

    lax.fori_loop(0, n_sub, body, 0)


def _gla(p0, v16, lf, lb, *, n_lat):
    rows = p0.shape[0]
    n_grp = rows // GLA_GROUP
    lat_grp = n_lat // GLA_GROUP
    G = GLA_GROUP
    qk_w = GLA_HEADS * GLA_DK
    v_w = GLA_HEADS * GLA_DV

    def fwd(c):
        return (c + lat_grp) % n_grp

    def bwd(c):
        return n_grp - 1 - c

    def spec(width, order, col):
        return pl.BlockSpec((G, width), lambda c: (order(c), col))

    in_specs = []
    for order in (fwd, bwd):
        in_specs += [spec(qk_w, order, 0), spec(qk_w, order, 1), spec(v_w, order, 0),
                     spec(qk_w, order, 0)]
    out_specs = [spec(v_w, fwd, 0), spec(v_w, bwd, 0)]
    return pl.pallas_call(
        _gla_kernel,
        grid=(n_grp,),
        in_specs=in_specs,
        out_specs=out_specs,
        out_shape=[jax.ShapeDtypeStruct((rows, v_w), F32)] * 2,
        scratch_shapes=([pltpu.VMEM((GLA_DK, GLA_DV), F32)] * (2 * GLA_HEADS)
                        + [pltpu.VMEM((2, GLA_CHUNK, GLA_DK), F32)] * (2 * GLA_HEADS)),
        compiler_params=_cparams(("arbitrary",), 32),
        name="gla_scan",
    )(p0, p0, v16, lf, p0, p0, v16, lb)


def _mix0_out_kernel(h_ref, mod_ref, of_ref, ob_ref, r_ref, uprev_ref, u_ref, unext_ref,
                     nw_ref, pw_ref, ps_ref, wout_ref, o_ref, ext_ref,
                     *, n_lat, n_ctx, tm):
    i = pl.program_id(0)
    lat_tiles = n_lat // tm
    n_tiles = (n_lat + n_ctx) // tm
    in_ctx = i >= lat_tiles
    is_first = (i == 0) | (i == lat_tiles)
    is_last = (i == lat_tiles - 1) | (i == n_tiles - 1)

    o = of_ref[...] + ob_ref[...]
    r = r_ref[...]
    heads = []
    for hd in range(GLA_HEADS):
        heads.append(_rms(o[:, hd * GLA_DV:(hd + 1) * GLA_DV]) * nw_ref[...])
    gl = jnp.concatenate(heads, axis=1) * (r * jax.nn.sigmoid(r))

    u = u_ref[...]
    H = POOL_HALO
    ext_ref[0:H, :] = jnp.where(is_first, 0.0, uprev_ref[...])
    ext_ref[H:H + tm, :] = u
    ext_ref[H + tm:2 * H + tm, :] = jnp.where(is_last, 0.0, unext_ref[...])
    seq_len = jnp.where(in_ctx, n_ctx, n_lat)
    t = (i - jnp.where(in_ctx, lat_tiles, 0)) * tm + lax.broadcasted_iota(jnp.int32, (tm, 1), 0)
    pooled = []
    for gi, w in enumerate(POOL_WINDOWS):
        cols = slice(gi * POOL_GC, (gi + 1) * POOL_GC)
        s = ext_ref[H - w // 2:H - w // 2 + tm, cols]
        for dlt in range(-w // 2 + 1, w - w // 2):
            s = s + ext_ref[H + dlt:H + dlt + tm, cols]
        lo = jnp.maximum(t - w // 2, 0)
        hi = jnp.minimum(t + (w - w // 2), seq_len)
        cnt = (hi - lo).astype(F32)
        pg = (s / cnt - u[:, cols]).astype(BF16)
        pooled.append(jnp.dot(pg, pw_ref[gi], preferred_element_type=F32) * ps_ref[:, cols])

    mix = jnp.concatenate([gl] + pooled, axis=1).astype(BF16)
    y = jnp.dot(mix, wout_ref[...], preferred_element_type=F32)
    is_ctx = _is_ctx_rows(i * tm, tm, n_lat)
    o_ref[...] = h_ref[...] + _mod_row(mod_ref, is_ctx, 2) * y


def _mix0_out(h, mod3, o_f, o_b, p0, norm_w, pool_w, pool_scale, w_out, *, n_lat, n_ctx):
    rows, d = h.shape
    tm = 256
    gw = GLA_HEADS * GLA_DV
    pwid = len(POOL_WINDOWS) * POOL_GC
    hb = tm // POOL_HALO
    n_hblk = rows // POOL_HALO
    r_col = (2 * GLA_HEADS * GLA_DK + gw) // gw
    u_col = (2 * GLA_HEADS * GLA_DK + 2 * gw) // pwid
    return pl.pallas_call(
        functools.partial(_mix0_out_kernel, n_lat=n_lat, n_ctx=n_ctx, tm=tm),
        grid=(rows // tm,),
        in_specs=[pl.BlockSpec((tm, d), lambda i: (i, 0)),
                  pl.BlockSpec((2, 3, d), lambda i: (0, 0, 0)),
                  pl.BlockSpec((tm, gw), lambda i: (i, 0)),
                  pl.BlockSpec((tm, gw), lambda i: (i, 0)),
                  pl.BlockSpec((tm, gw), lambda i: (i, r_col)),
                  pl.BlockSpec((POOL_HALO, pwid), lambda i: (jnp.maximum(i * hb - 1, 0), u_col)),
                  pl.BlockSpec((tm, pwid), lambda i: (i, u_col)),
                  pl.BlockSpec((POOL_HALO, pwid),
                               lambda i: (jnp.minimum((i + 1) * hb, n_hblk - 1), u_col)),
                  pl.BlockSpec((1, GLA_DV), lambda i: (0, 0)),
                  pl.BlockSpec((len(POOL_WINDOWS), POOL_GC, POOL_GC), lambda i: (0, 0, 0)),
                  pl.BlockSpec((1, pwid), lambda i: (0, 0)),
                  pl.BlockSpec((d, d), lambda i: (0, 0))],
        out_specs=pl.BlockSpec((tm, d), lambda i: (i, 0)),
        out_shape=jax.ShapeDtypeStruct((rows, d), F32),
        scratch_shapes=[pltpu.VMEM((tm + 2 * POOL_HALO, pwid), F32)],
        compiler_params=_cparams(("arbitrary",), 48),
        name="mix0_readout",
    )(h, mod3, o_f, o_b, p0, p0, p0, p0, norm_w.reshape(1, GLA_DV), pool_w,
      pool_scale.reshape(1, pwid), w_out)


def _proj1_kernel(h_ref, mod_ref, w_ref, cos_ref, sin_ref, qk_ref, vt_ref, z_ref,
                  *, n_lat, tm, tn, n_qt, qscale):
    i = pl.program_id(0)
    j = pl.program_id(1)

    @pl.when(j == 0)
    def _():
        _store_modulated(h_ref, mod_ref, z_ref, i * tm, tm, n_lat)

    @pl.when(j < 2 * n_qt)
    def _():
        acc = jnp.dot(z_ref[...], w_ref[...], preferred_element_type=F32)
        scale = jnp.where(j < n_qt, qscale, 1.0)
        cs = cos_ref[...] * scale
        sn = sin_ref[...] * scale
        for gidx in range(tn // LANES):
            x = acc[:, gidx * LANES:(gidx + 1) * LANES]
            y = x * cs + pltpu.roll(x, LANES // 2, 1) * sn
            qk_ref[:, gidx * LANES:(gidx + 1) * LANES] = y.astype(BF16)

    @pl.when(j >= 2 * n_qt)
    def _():
        vt = jnp.dot(z_ref[...], w_ref[...], preferred_element_type=F32).T
        for hd in range(tn // DIFF_DV):
            vt_ref[hd] = vt[hd * DIFF_DV:(hd + 1) * DIFF_DV].astype(BF16)


def _proj1(h, mod3, w_qkv, cos_t, sin_t, *, tm, n_lat):
    rows, d = h.shape
    tn = 1024
    n_qt = d // tn
    n_vt = w_qkv.shape[1] // tn - 2 * n_qt
    hpt = tn // DIFF_DV
    qscale = (DIFF_DH ** -0.5) * math.log2(math.e)
    return pl.pallas_call(
        functools.partial(_proj1_kernel, n_lat=n_lat, tm=tm, tn=tn, n_qt=n_qt, qscale=qscale),
        grid=(rows // tm, 2 * n_qt + n_vt),
        in_specs=[pl.BlockSpec((tm, d), lambda i, j: (i, 0)),
                  pl.BlockSpec((2, 3, d), lambda i, j: (0, 0, 0)),
                  pl.BlockSpec((d, tn), lambda i, j: (0, j)),
                  pl.BlockSpec((tm, LANES), lambda i, j: (i, 0)),
                  pl.BlockSpec((tm, LANES), lambda i, j: (i, 0))],
        out_specs=[pl.BlockSpec((tm, tn), lambda i, j: (i, jnp.minimum(j, 2 * n_qt - 1))),
                   pl.BlockSpec((hpt, None, DIFF_DV, tm),
                                lambda i, j: (jnp.maximum(j - 2 * n_qt, 0), i, 0, 0))],
        out_shape=[jax.ShapeDtypeStruct((rows, 2 * d), BF16),
                   jax.ShapeDtypeStruct((DIFF_HEADS, rows // tm, DIFF_DV, tm), BF16)],
        scratch_shapes=[pltpu.VMEM((tm, d), BF16)],
        compiler_params=_cparams(("arbitrary", "arbitrary"), 52),
        name="proj_qkv_rope",
    )(h, mod3, w_qkv, cos_t, sin_t)


ATT_TILE = 256


def _attn_kernel(lam_ref, q_ref, k_ref, vt_ref, nw_ref, o_ref, m_ref, l_ref, acc_ref,
                 sa_ref, sb_ref, ca_ref, cb_ref, p_ref, *, tk, n_kv, lam_init):
    tq = q_ref.shape[0]
    T = ATT_TILE
    SUB = 8
    m_ref[...] = jnp.full_like(m_ref, -jnp.inf)
    l_ref[...] = jnp.zeros_like(l_ref)
    acc_ref[...] = jnp.zeros_like(acc_ref)

    units = [(c, qb) for qb in range(tq // T) for c in range(2)]

    def scores(jj, s_ref, c_ref, c, qb):
        comp = slice(c * DIFF_DH, (c + 1) * DIFF_DH)
        cols = slice(qb * T, (qb + 1) * T)
        qc = q_ref[cols, comp]
        cmax = None
        for kb in range(tk // T):
            kc = k_ref[jj * tk + kb * T:jj * tk + (kb + 1) * T, comp]
            st = lax.dot_general(kc, qc, (((1,), (1,)), ((), ())), preferred_element_type=F32)
            s_ref[c, kb * T:(kb + 1) * T, cols] = st
            bmax = jnp.max(st.reshape(T // SUB, SUB, T), axis=0)
            cmax = bmax if cmax is None else jnp.maximum(cmax, bmax)
        c_ref[c, :, cols] = cmax

    def accumulate(jj, s_ref, c_ref, c, qb):
        cols = slice(qb * T, (qb + 1) * T)
        m_old = m_ref[c, :, cols]
        m_new = jnp.maximum(m_old, jnp.max(c_ref[c, :, cols], axis=0, keepdims=True))
        alpha = jnp.exp2(m_old - m_new)
        lsum = jnp.zeros((SUB, T), F32)
        for kb in range(tk // T):
            keys = slice(kb * T, (kb + 1) * T)
            pt = jnp.exp2(s_ref[c, keys, cols] - m_new)
            lsum = lsum + jnp.sum(pt.reshape(T // SUB, SUB, T), axis=0)
            p_ref[c, keys, cols] = pt.astype(BF16)
        pv = jnp.dot(vt_ref[jj], p_ref[c, :, cols], preferred_element_type=F32)
        l_ref[c, :, cols] = alpha * l_ref[c, :, cols] + lsum
        acc_ref[c, :, cols] = alpha * acc_ref[c, :, cols] + pv
        m_ref[c, :, cols] = m_new

    slots = ((sa_ref, ca_ref), (sb_ref, cb_ref))
    for c, qb in units:
        scores(0, *slots[0], c, qb)
    for jj in range(n_kv):
        for c, qb in units:
            accumulate(jj, *slots[jj % 2], c, qb)
            if jj + 1 < n_kv:
                scores(jj + 1, *slots[(jj + 1) % 2], c, qb)

    lp = lam_ref[...]
    lam = (jnp.exp(jnp.sum(lp[0:1] * lp[1:2], axis=-1, keepdims=True))
           - jnp.exp(jnp.sum(lp[2:3] * lp[3:4], axis=-1, keepdims=True)) + lam_init)
    l0 = jnp.sum(l_ref[0], axis=0, keepdims=True)
    l1 = jnp.sum(l_ref[1], axis=0, keepdims=True)
    ot = acc_ref[0] / l0 - lam * (acc_ref[1] / l1)
    ms = jnp.mean(ot * ot, axis=0, keepdims=True)
    ont = ot * lax.rsqrt(ms + NORM_EPS) * (nw_ref[...] * (1.0 - lam_init))
    o_ref[...] = ont.T.astype(BF16)


def _diff_attn(qk, vt, lam_p, norm_w, *, n_lat, lam_init):
    rows = qk.shape[0]
    d = DIFF_HEADS * DIFF_DV
    n_kv, tk = vt.shape[1], vt.shape[3]
    tq = 512
    return pl.pallas_call(
        functools.partial(_attn_kernel, tk=tk, n_kv=n_kv, lam_init=lam_init),
        grid=(DIFF_HEADS, n_lat // tq),
        in_specs=[pl.BlockSpec((4, DIFF_DH), lambda h, i: (0, 0)),
                  pl.BlockSpec((tq, DIFF_DV), lambda h, i: (i, h)),
                  pl.BlockSpec((rows, DIFF_DV), lambda h, i: (0, DIFF_HEADS + h)),
                  pl.BlockSpec((None, n_kv, DIFF_DV, tk), lambda h, i: (h, 0, 0, 0)),
                  pl.BlockSpec((DIFF_DV, 1), lambda h, i: (0, 0))],
        out_specs=pl.BlockSpec((tq, DIFF_DV), lambda h, i: (i, h)),
        out_shape=jax.ShapeDtypeStruct((n_lat, d), BF16),
        scratch_shapes=[pltpu.VMEM((2, 1, tq), F32), pltpu.VMEM((2, 8, tq), F32),
                        pltpu.VMEM((2, DIFF_DV, tq), F32),
                        pltpu.VMEM((2, tk, tq), F32), pltpu.VMEM((2, tk, tq), F32),
                        pltpu.VMEM((2, 8, tq), F32), pltpu.VMEM((2, 8, tq), F32),
                        pltpu.VMEM((2, tk, tq), BF16)],
        compiler_params=_cparams(("arbitrary", "arbitrary"), 48),
        name="diff_attn",
    )(lam_p, qk, qk, vt, norm_w.reshape(DIFF_DV, 1))


def _outproj_kernel(h_ref, mod_ref, x_ref, w_ref, o_ref):
    y = jnp.dot(x_ref[...], w_ref[...], preferred_element_type=F32)
    o_ref[...] = h_ref[...] + mod_ref[0, 2:3, :] * y


def _outproj_latent(h, mod3, x, w, *, n_lat):
    d = h.shape[1]
    tm = 512
    return pl.pallas_call(
        _outproj_kernel,
        grid=(n_lat // tm,),
        in_specs=[pl.BlockSpec((tm, d), lambda i: (i, 0)),
                  pl.BlockSpec((2, 3, d), lambda i: (0, 0, 0)),
                  pl.BlockSpec((tm, d), lambda i: (i, 0)),
                  pl.BlockSpec((d, d), lambda i: (0, 0))],
        out_specs=pl.BlockSpec((tm, d), lambda i: (i, 0)),
        out_shape=jax.ShapeDtypeStruct((n_lat, d), F32),
        compiler_params=_cparams(("arbitrary",), 40),
        name="attn_outproj",
    )(h, mod3, x, w)


def _rope_tables(n_lat, n_ctx):
    rows = n_lat // GRID_W
    row = jnp.repeat(jnp.arange(rows), GRID_W).astype(F32)
    col = jnp.tile(jnp.arange(GRID_W), rows).astype(F32)
    n_freq = DIFF_DH // 4
    inv_freq = ROPE_THETA ** (-jnp.arange(n_freq, dtype=F32) / n_freq)
    ang = jnp.concatenate([row[:, None] * inv_freq, col[:, None] * inv_freq], axis=-1)
    cos, sin = jnp.cos(ang), jnp.sin(ang)
    cos_t = jnp.concatenate([cos, cos], axis=-1)
    sin_t = jnp.concatenate([-sin, sin], axis=-1)
    cos_t = jnp.concatenate([cos_t, jnp.ones((n_ctx, DIFF_DH), F32)], axis=0)
    sin_t = jnp.concatenate([sin_t, jnp.zeros((n_ctx, DIFF_DH), F32)], axis=0)
    return cos_t, sin_t


def kernel(x, c, ctx, c_ctx, ada_w, ada_b, ffn_w1, ffn_w3, ffn_w2, gla_w_in, gla_gate_w2,
           gla_gate_b, gla_norm_w, pool_w, pool_scale, mix0_w_out, diff_w_qkv, diff_lambda,
           diff_norm_w, diff_w_out, final_norm_w):
    assert x.shape[0] == 1 and ada_w.shape[0] == 2
    n_lat, d = x.shape[1], x.shape[2]
    n_ctx = ctx.shape[1]
    rows = n_lat + n_ctx
    tm = 768
    assert rows % tm == 0 and n_lat % 512 == 0 and n_ctx % 256 == 0 and n_lat % GRID_W == 0

    h = jnp.concatenate([x[0], ctx[0]], axis=0)
    c8 = jnp.zeros((8, d), F32).at[0].set(c[0]).at[1].set(c_ctx)
    mods = _ada(c8, ada_w, ada_b)[:, :2].reshape(2, 2, 9, d)

    def first_weights(layer, half):
        return tuple(wt[layer, half].astype(BF16) for wt in (ffn_w1, ffn_w3, ffn_w2))

    def following(layer, half):
        return (ffn_w1, ffn_w3, ffn_w2, layer, half)

    m = mods[0]
    h, w_next = _ffn(h, m[:, 0:3], first_weights(0, 0), rows=rows, tm=tm, n_lat=n_lat,
                     nxt=following(0, 1))
    w_in = gla_w_in[0]
    n_qkvr = 2 * GLA_HEADS * GLA_DK + 2 * GLA_HEADS * GLA_DV
    n_gate = 2 * GLA_GATE_RANK
    w_a = w_in[:, :n_qkvr].astype(BF16)
    w_u = w_in[:, n_qkvr + n_gate:].astype(BF16)
    w_gate = jnp.pad(w_in[:, n_qkvr:n_qkvr + n_gate], ((0, 0), (0, LANES - n_gate))).astype(BF16)
    gw2 = gla_gate_w2[0].astype(BF16)
    gw2p = jnp.zeros((2, LANES, gw2.shape[-1]), BF16)
    gw2p = gw2p.at[0, :GLA_GATE_RANK].set(gw2[0]).at[1, GLA_GATE_RANK:n_gate].set(gw2[1])
    p0, v16, lf, lb = _proj0(h, m[:, 3:6], w_a, w_u, w_gate, gw2p, gla_gate_b[0], tm=tm,
                             n_lat=n_lat)
    o_f, o_b = _gla(p0, v16, lf, lb, n_lat=n_lat)
    h = _mix0_out(h, m[:, 3:6], o_f, o_b, p0, gla_norm_w[0], pool_w[0].astype(BF16),
                  pool_scale[0], mix0_w_out[0].astype(BF16), n_lat=n_lat, n_ctx=n_ctx)
    h, w_next = _ffn(h, m[:, 6:9], w_next, rows=rows, tm=tm, n_lat=n_lat, nxt=following(1, 0))

    m = mods[1]
    h, w_next = _ffn(h, m[:, 0:3], w_next, rows=rows, tm=tm, n_lat=n_lat, nxt=following(1, 1))
    cos_t, sin_t = _rope_tables(n_lat, n_ctx)
    qk, vt = _proj1(h, m[:, 3:6], diff_w_qkv[0].astype(BF16), cos_t, sin_t, tm=tm, n_lat=n_lat)
    lam_init = 0.8 - 0.6 * math.exp(-0.3 * 1)
    o = _diff_attn(qk, vt, diff_lambda[0], diff_norm_w[0], n_lat=n_lat, lam_init=lam_init)
    hl = _outproj_latent(h, m[:, 3:6], o, diff_w_out[0].astype(BF16), n_lat=n_lat)
    out = _ffn(hl, m[:, 6:9], w_next, rows=n_lat, tm=512, n_lat=n_lat, final_w=final_norm_w)
    return out[None]
```

```python
import functools
import math

import jax
import jax.numpy as jnp
from jax import lax
from jax.experimental import pallas as pl
from jax.experimental.pallas import tpu as pltpu

F32 = jnp.float32
BF16 = jnp.bfloat16

NORM_EPS = 1e-6
GRID_W = 64
ROPE_THETA = 10000.0

GLA_HEADS = 4
GLA_DK = 128
GLA_DV = 256
GLA_GATE_RANK = 16
GLA_TAU = 16.0
GLA_CHUNK = 64
GLA_SUB = 16
GLA_GROUP = 256
POOL_WINDOWS = (2, 4, 8, 16)
POOL_GC = 256
POOL_HALO = 8

DIFF_HEADS = 8
DIFF_DH = 128
DIFF_DV = 256

LANES = 128
MIB = 1024 * 1024


def _cparams(semantics, vmem_mib):
    return pltpu.CompilerParams(dimension_semantics=semantics,
                                vmem_limit_bytes=int(vmem_mib * MIB))


def _rms(x):
    return x * lax.rsqrt(jnp.mean(x * x, axis=-1, keepdims=True) + NORM_EPS)


def _is_ctx_rows(row0, rows, n_lat):
    return (row0 + lax.broadcasted_iota(jnp.int32, (rows, 1), 0)) >= n_lat


def _mod_row(mod_ref, is_ctx, k):
    return jnp.where(is_ctx, mod_ref[1, k:k + 1, :], mod_ref[0, k:k + 1, :])


def _modulated(h, mod_ref, is_ctx):
    return _rms(h) * (1.0 + _mod_row(mod_ref, is_ctx, 1)) + _mod_row(mod_ref, is_ctx, 0)


NORM_ROWS = 32


def _store_modulated(h_ref, mod_ref, z_ref, row0, tm, n_lat):
    def body(r, carry):
        start = pl.multiple_of(r * NORM_ROWS, NORM_ROWS)
        rows = pl.ds(start, NORM_ROWS)
        which = jnp.where(row0 + start >= n_lat, 1, 0)
        shift = mod_ref[which, 0:1, :]
        scale = mod_ref[which, 1:2, :]
        z_ref[rows, :] = (_rms(h_ref[rows, :]) * (1.0 + scale) + shift).astype(BF16)
        return carry

    lax.fori_loop(0, tm // NORM_ROWS, body, 0, unroll=4)


def _ada_kernel(c_ref, w_ref, b_ref, o_ref):
    c = c_ref[...]
    s = (c * jax.nn.sigmoid(c)).astype(BF16)
    o_ref[0] = jnp.dot(s, w_ref[0].astype(BF16), preferred_element_type=F32) + b_ref[0]


def _ada(c8, ada_w, ada_b):
    depth, d, n = ada_w.shape
    tn = 1024
    return pl.pallas_call(
        _ada_kernel,
        grid=(depth, n // tn),
        in_specs=[pl.BlockSpec((8, d), lambda i, j: (0, 0)),
                  pl.BlockSpec((1, d, tn), lambda i, j: (i, 0, j)),
                  pl.BlockSpec((1, 1, tn), lambda i, j: (i, 0, j))],
        out_specs=pl.BlockSpec((1, 8, tn), lambda i, j: (i, 0, j)),
        out_shape=jax.ShapeDtypeStruct((depth, 8, n), F32),
        compiler_params=_cparams(("arbitrary", "arbitrary"), 40),
        name="ada_mod",
    )(c8, ada_w, ada_b.reshape(depth, 1, n))


def _ffn_kernel(*refs, n_lat, tm, final, starts):
    if final:
        h_ref, mod_ref, w1_ref, w3_ref, w2_ref, fnw_ref, o_ref, z_ref = refs
        casts = ()
    else:
        (h_ref, mod_ref, w1_ref, w3_ref, w2_ref, n1_ref, n3_ref, n2_ref,
         o_ref, c1_ref, c3_ref, c2_ref, z_ref) = refs
        casts = ((n1_ref, c1_ref), (n3_ref, c3_ref), (n2_ref, c2_ref))
    i = pl.program_id(0)
    j = pl.program_id(1)
    is_ctx = _is_ctx_rows(i * tm, tm, n_lat)

    @pl.when(j == 0)
    def _():
        _store_modulated(h_ref, mod_ref, z_ref, i * tm, tm, n_lat)
        o_ref[...] = jnp.zeros_like(o_ref)

    step = i * pl.num_programs(1) + j
    for m, (src_ref, dst_ref) in enumerate(casts):
        @pl.when((step >= starts[m]) & (step < starts[m + 1]))
        def _(src_ref=src_ref, dst_ref=dst_ref):
            dst_ref[...] = src_ref[...].astype(BF16)

    z = z_ref[...]
    u = jnp.dot(z, w1_ref[...], preferred_element_type=F32)
    g = jnp.dot(z, w3_ref[...], preferred_element_type=F32)
    a = (u * jax.nn.sigmoid(u) * g).astype(BF16)
    o_ref[...] += jnp.dot(a, w2_ref[...], preferred_element_type=F32)

    @pl.when(j == pl.num_programs(1) - 1)
    def _():
        def body(r, carry):
            start = pl.multiple_of(r * NORM_ROWS, NORM_ROWS)
            rows = pl.ds(start, NORM_ROWS)
            which = jnp.where(i * tm + start >= n_lat, 1, 0)
            hn = h_ref[rows, :] + 0.5 * mod_ref[which, 2:3, :] * o_ref[rows, :]
            if final:
                hn = _rms(hn) * fnw_ref[...]
            o_ref[rows, :] = hn
            return carry

        lax.fori_loop(0, tm // NORM_ROWS, body, 0, unroll=4)


def _ffn(h, mod3, w, *, rows, tm, n_lat, nxt=None, final_w=None):
    w1, w3, w2 = w
    d = h.shape[1]
    dff = w1.shape[-1]
    tf = 512
    nj = dff // tf
    final = final_w is not None
    in_specs = [pl.BlockSpec((tm, d), lambda i, j: (i, 0)),
                pl.BlockSpec((2, 3, d), lambda i, j: (0, 0, 0)),
                pl.BlockSpec((d, tf), lambda i, j: (0, j)),
                pl.BlockSpec((d, tf), lambda i, j: (0, j)),
                pl.BlockSpec((tf, d), lambda i, j: (j, 0))]
    args = [h, mod3, w1, w3, w2]
    out_specs = [pl.BlockSpec((tm, d), lambda i, j: (i, 0))]
    out_shape = [jax.ShapeDtypeStruct((rows, d), F32)]
    starts = ()
    if final:
        in_specs.append(pl.BlockSpec((1, d), lambda i, j: (0, 0)))
        args.append(final_w.reshape(1, d))
    else:
        f1, f3, f2, layer, half = nxt
        cc = tf
        per_row = dff // cc
        steps = (rows // tm) * nj
        cr_up, cr_down = next((a, b) for a, b in ((d // 4, d // 2), (d // 2, d // 2), (d, d))
                              if (2 * (d // a) + d // b) * per_row <= steps)
        counts = ((d // cr_up) * per_row, (d // cr_up) * per_row, (d // cr_down) * per_row)
        starts = (0, counts[0], counts[0] + counts[1], sum(counts))

        def blk(i, j, m):
            return jnp.clip(i * nj + j - starts[m], 0, counts[m] - 1)

        def up_map(m):
            return lambda i, j: (blk(i, j, m) // per_row, blk(i, j, m) % per_row)

        def down_map(m):
            return lambda i, j: (blk(i, j, m) % per_row, blk(i, j, m) // per_row)

        def stacked(index_map):
            return lambda i, j: (layer, half) + index_map(i, j)

        in_specs += [pl.BlockSpec((None, None, cr_up, cc), stacked(up_map(0))),
                     pl.BlockSpec((None, None, cr_up, cc), stacked(up_map(1))),
                     pl.BlockSpec((None, None, cc, cr_down), stacked(down_map(2)))]
        args += [f1, f3, f2]
        out_specs += [pl.BlockSpec((cr_up, cc), up_map(0)), pl.BlockSpec((cr_up, cc), up_map(1)),
                      pl.BlockSpec((cc, cr_down), down_map(2))]
        out_shape += [jax.ShapeDtypeStruct((d, dff), BF16), jax.ShapeDtypeStruct((d, dff), BF16),
                      jax.ShapeDtypeStruct((dff, d), BF16)]
    res = pl.pallas_call(
        functools.partial(_ffn_kernel, n_lat=n_lat, tm=tm, final=final, starts=starts),
        grid=(rows // tm, nj),
        in_specs=in_specs,
        out_specs=out_specs,
        out_shape=out_shape,
        scratch_shapes=[pltpu.VMEM((tm, d), BF16)],
        compiler_params=_cparams(("arbitrary", "arbitrary"), 56 if final else 60),
        name="ffn_final" if final else "ffn_half",
    )(*args)
    return res[0] if final else (res[0], tuple(res[1:]))


def _log_sigmoid(x):
    return jnp.minimum(x, 0.0) - jnp.log1p(jnp.exp(-jnp.abs(x)))


def _proj0_kernel(h_ref, mod_ref, wa_ref, wu_ref, wg_ref, gw2_ref, gb_ref,
                  p_ref, v16_ref, lf_ref, lb_ref, z_ref, *, n_lat, tm, n_a, v_tiles):
    i = pl.program_id(0)
    j = pl.program_id(1)

    is_v = (j >= v_tiles[0]) & (j < v_tiles[1])

    @pl.when(j == 0)
    def _():
        _store_modulated(h_ref, mod_ref, z_ref, i * tm, tm, n_lat)
        z = z_ref[...]
        p_ref[...] = jnp.dot(z, wa_ref[...], preferred_element_type=F32)
        gz = jnp.dot(z, wg_ref[...], preferred_element_type=F32).astype(BF16)
        for d, out in ((0, lf_ref), (1, lb_ref)):
            zz = jnp.dot(gz, gw2_ref[d], preferred_element_type=F32) + gb_ref[d]
            out[...] = _log_sigmoid(zz) / GLA_TAU

    @pl.when((j > 0) & (j < n_a) & jnp.logical_not(is_v))
    def _():
        p_ref[...] = jnp.dot(z_ref[...], wa_ref[...], preferred_element_type=F32)

    @pl.when(is_v)
    def _():
        acc = jnp.dot(z_ref[...], wa_ref[...], preferred_element_type=F32)
        p_ref[...] = acc
        v16_ref[...] = acc.astype(BF16)

    @pl.when(j >= n_a)
    def _():
        p_ref[...] = jnp.dot(z_ref[...], wu_ref[...], preferred_element_type=F32)


def _proj0(h, mod3, w_a, w_u, w_gate, gate_w2p, gate_b, *, tm, n_lat):
    rows, d = h.shape
    qk = gate_b.shape[-1]
    tn = 1024
    n_a, n_u = w_a.shape[1] // tn, w_u.shape[1] // tn
    nj = n_a + n_u
    v_lo, v_w = 2 * qk, GLA_HEADS * GLA_DV
    v_tiles = (v_lo // tn, (v_lo + v_w) // tn)

    def v_map(i, j):
        return (i, jnp.clip(j - v_tiles[0], 0, v_tiles[1] - v_tiles[0] - 1))

    return pl.pallas_call(
        functools.partial(_proj0_kernel, n_lat=n_lat, tm=tm, n_a=n_a, v_tiles=v_tiles),
        grid=(rows // tm, nj),
        in_specs=[pl.BlockSpec((tm, d), lambda i, j: (i, 0)),
                  pl.BlockSpec((2, 3, d), lambda i, j: (0, 0, 0)),
                  pl.BlockSpec((d, tn), lambda i, j: (0, jnp.minimum(j, n_a - 1))),
                  pl.BlockSpec((d, tn), lambda i, j: (0, jnp.maximum(j - n_a, 0))),
                  pl.BlockSpec((d, LANES), lambda i, j: (0, 0)),
                  pl.BlockSpec((2, LANES, qk), lambda i, j: (0, 0, 0)),
                  pl.BlockSpec((2, 1, qk), lambda i, j: (0, 0, 0))],
        out_specs=[pl.BlockSpec((tm, tn), lambda i, j: (i, j)),
                   pl.BlockSpec((tm, tn), v_map),
                   pl.BlockSpec((tm, qk), lambda i, j: (i, 0)),
                   pl.BlockSpec((tm, qk), lambda i, j: (i, 0))],
        out_shape=[jax.ShapeDtypeStruct((rows, nj * tn), F32),
                   jax.ShapeDtypeStruct((rows, v_w), BF16),
                   jax.ShapeDtypeStruct((rows, qk), F32),
                   jax.ShapeDtypeStruct((rows, qk), F32)],
        scratch_shapes=[pltpu.VMEM((tm, d), BF16)],
        compiler_params=_cparams(("arbitrary", "arbitrary"), 56),
        name="proj_gla_pool",
    )(h, mod3, w_a, w_u, w_gate, gate_w2p, gate_b.reshape(2, 1, qk))


def _split3(x):
    x1 = x.astype(BF16)
    r1 = x - x1.astype(F32)
    x2 = r1.astype(BF16)
    x3 = (r1 - x2.astype(F32)).astype(BF16)
    return x1, x2, x3


def _gla_chunk(q_ref, k_ref, v_ref, g_ref, o_ref, s_ref, b_ref, base, head, reverse):
    C, SB = GLA_CHUNK, GLA_SUB
    nb = C // SB
    rows = pl.ds(base, C)
    kcols = slice(head * GLA_DK, (head + 1) * GLA_DK)
    vcols = slice(head * GLA_DV, (head + 1) * GLA_DV)
    q = q_ref[rows, kcols] * (GLA_DK ** -0.5)
    k = k_ref[rows, kcols]
    v = v_ref[rows, vcols]
    g = g_ref[rows, kcols]

    ti = lax.broadcasted_iota(jnp.int32, (C, C), 0)
    si = lax.broadcasted_iota(jnp.int32, (C, C), 1)
    tri = jnp.where((si >= ti) if reverse else (si <= ti), 1.0, 0.0).astype(BF16)
    b = sum(jnp.dot(tri, part, preferred_element_type=F32) for part in _split3(g))
    yield

    b_ref[0] = b
    b_ref[1] = k
    s_old = s_ref[...]
    inter = jnp.dot((q * jnp.exp(b)).astype(BF16), s_old.astype(BF16),
                    preferred_element_type=F32)
    last = 0 if reverse else C - 1
    b_last = b_ref[0, last:last + 1, :]
    ke = (k * jnp.exp(b_last - b)).astype(BF16)
    upd = lax.dot_general(ke, v, (((0,), (0,)), ((), ())), preferred_element_type=F32)
    raws = []
    for i in range(nb):
        blk = slice(i * SB, (i + 1) * SB)
        edge = (i + 1) * SB if reverse else i * SB - 1
        has_off = (i < nb - 1) if reverse else (i > 0)
        if has_off:
            r_i = b_ref[0, edge:edge + 1, :]
            qe = (q[blk] * jnp.exp(b[blk] - r_i)).astype(BF16)
            ke_i = (k * jnp.exp(jnp.minimum(r_i - b, 0.0))).astype(BF16)
            raws.append(lax.dot_general(qe, ke_i, (((1,), (1,)), ((), ())),
                                        preferred_element_type=F32))
        else:
            raws.append(None)
    yield

    trow = lax.broadcasted_iota(jnp.int32, (SB, LANES), 0)
    ones = jnp.ones((LANES, LANES), BF16)
    dsums = []
    for i in range(nb):
        blk = slice(i * SB, (i + 1) * SB)
        q_i = q[blk]
        b_i = b[blk]
        terms = []
        for s in range(SB):
            row = i * SB + s
            k_s = b_ref[1, row:row + 1, :]
            b_s = b_ref[0, row:row + 1, :]
            keep = (trow <= s) if reverse else (trow >= s)
            terms.append(jnp.where(keep, q_i * k_s * jnp.exp(b_i - b_s), 0.0))
        stacked = jnp.concatenate(terms, axis=0).astype(BF16)
        dsums.append(jnp.dot(stacked, ones, preferred_element_type=F32))
    yield

    col = lax.broadcasted_iota(jnp.int32, (SB, C), 1)
    lane = lax.broadcasted_iota(jnp.int32, (SB, LANES), 1)
    a_rows = []
    for i in range(nb):
        diag = jnp.zeros((SB, LANES), F32)
        for s in range(SB):
            diag = jnp.where(lane == i * SB + s, dsums[i][s * SB:(s + 1) * SB], diag)
        a_i = diag[:, :C]
        if raws[i] is not None:
            off_mask = (col >= (i + 1) * SB) if reverse else (col < i * SB)
            a_i = a_i + jnp.where(off_mask, raws[i], 0.0)
        a_rows.append(a_i)
    attn = jnp.concatenate(a_rows, axis=0).astype(BF16)
    intra = jnp.dot(attn, v, preferred_element_type=F32)
    decay = jnp.broadcast_to(jnp.exp(b_last), (GLA_DK, GLA_DK)).T
    s_ref[...] = jnp.concatenate([decay, decay], axis=1) * s_old + upd
    yield

    o_ref[rows, vcols] = inter + intra
    yield


GLA_STAGES = 5


def _gla_kernel(qf_ref, kf_ref, vf_ref, gf_ref, qb_ref, kb_ref, vb_ref, gb_ref,
                of_ref, ob_ref, *scratch):
    n_chain = 2 * GLA_HEADS
    state, stage = scratch[:n_chain], scratch[n_chain:]

    @pl.when(pl.program_id(0) == 0)
    def _():
        for s_ref in state:
            s_ref[...] = jnp.zeros_like(s_ref)

    n_sub = GLA_GROUP // GLA_CHUNK

    def body(c, carry):
        fbase = pl.multiple_of(c * GLA_CHUNK, GLA_CHUNK)
        rbase = pl.multiple_of((n_sub - 1 - c) * GLA_CHUNK, GLA_CHUNK)
        chains = []
        for head in range(GLA_HEADS):
            f, r = head, GLA_HEADS + head
            chains.append(_gla_chunk(qf_ref, kf_ref, vf_ref, gf_ref, of_ref, state[f], stage[f],
                                     fbase, head, False))
            chains.append(_gla_chunk(qb_ref, kb_ref, vb_ref, gb_ref, ob_ref, state[r], stage[r],
                                     rbase, head, True))
        for _ in range(GLA_STAGES):
            for chain in chains:
                next(chain)
        return carry

    lax.fori_loop(0, n_sub, body, 0)


def _gla(p0, v16, lf, lb, *, n_lat):
    rows = p0.shape[0]
    n_grp = rows // GLA_GROUP
    lat_grp = n_lat // GLA_GROUP
    G = GLA_GROUP
    qk_w = GLA_HEADS * GLA_DK
    v_w = GLA_HEADS * GLA_DV

    def fwd(c):
        return (c + lat_grp) % n_grp

    def bwd(c):
        return n_grp - 1 - c

    def spec(width, order, col):
        return pl.BlockSpec((G, width), lambda c: (order(c), col))

    in_specs = []
    for order in (fwd, bwd):
        in_specs += [spec(qk_w, order, 0), spec(qk_w, order, 1), spec(v_w, order, 0),
                     spec(qk_w, order, 0)]
    out_specs = [spec(v_w, fwd, 0), spec(v_w, bwd, 0)]
    return pl.pallas_call(
        _gla_kernel,
        grid=(n_grp,),
        in_specs=in_specs,
        out_specs=out_specs,
        out_shape=[jax.ShapeDtypeStruct((rows, v_w), F32)] * 2,
        scratch_shapes=([pltpu.VMEM((GLA_DK, GLA_DV), F32)] * (2 * GLA_HEADS)
                        + [pltpu.VMEM((2, GLA_CHUNK, GLA_DK), F32)] * (2 * GLA_HEADS)),
        compiler_params=_cparams(("arbitrary",), 32),
        name="gla_scan",
    )(p0, p0, v16, lf, p0, p0, v16, lb)


def _mix0_out_kernel(h_ref, mod_ref, of_ref, ob_ref, r_ref, uprev_ref, u_ref, unext_ref,
                     nw_ref, pw_ref, ps_ref, wout_ref, o_ref, ext_ref,
                     *, n_lat, n_ctx, tm):
    i = pl.program_id(0)
    lat_tiles = n_lat // tm
    n_tiles = (n_lat + n_ctx) // tm
    in_ctx = i >= lat_tiles
    is_first = (i == 0) | (i == lat_tiles)
    is_last = (i == lat_tiles - 1) | (i == n_tiles - 1)

    o = of_ref[...] + ob_ref[...]
    r = r_ref[...]
    heads = []
    for hd in range(GLA_HEADS):
        heads.append(_rms(o[:, hd * GLA_DV:(hd + 1) * GLA_DV]) * nw_ref[...])
    gl = jnp.concatenate(heads, axis=1) * (r * jax.nn.sigmoid(r))

    u = u_ref[...]
    H = POOL_HALO
    ext_ref[0:H, :] = jnp.where(is_first, 0.0, uprev_ref[...])
    ext_ref[H:H + tm, :] = u
    ext_ref[H + tm:2 * H + tm, :] = jnp.where(is_last, 0.0, unext_ref[...])
    seq_len = jnp.where(in_ctx, n_ctx, n_lat)
    t = (i - jnp.where(in_ctx, lat_tiles, 0)) * tm + lax.broadcasted_iota(jnp.int32, (tm, 1), 0)
    pooled = []
    for gi, w in enumerate(POOL_WINDOWS):
        cols = slice(gi * POOL_GC, (gi + 1) * POOL_GC)
        s = ext_ref[H - w // 2:H - w // 2 + tm, cols]
        for dlt in range(-w // 2 + 1, w - w // 2):
            s = s + ext_ref[H + dlt:H + dlt + tm, cols]
        lo = jnp.maximum(t - w // 2, 0)
        hi = jnp.minimum(t + (w - w // 2), seq_len)
        cnt = (hi - lo).astype(F32)
        pg = (s / cnt - u[:, cols]).astype(BF16)
        pooled.append(jnp.dot(pg, pw_ref[gi], preferred_element_type=F32) * ps_ref[:, cols])

    mix = jnp.concatenate([gl] + pooled, axis=1).astype(BF16)
    y = jnp.dot(mix, wout_ref[...], preferred_element_type=F32)
    is_ctx = _is_ctx_rows(i * tm, tm, n_lat)
    o_ref[...] = h_ref[...] + _mod_row(mod_ref, is_ctx, 2) * y


def _mix0_out(h, mod3, o_f, o_b, p0, norm_w, pool_w, pool_scale, w_out, *, n_lat, n_ctx):
    rows, d = h.shape
    tm = 256
    gw = GLA_HEADS * GLA_DV
    pwid = len(POOL_WINDOWS) * POOL_GC
    hb = tm // POOL_HALO
    n_hblk = rows // POOL_HALO
    r_col = (2 * GLA_HEADS * GLA_DK + gw) // gw
    u_col = (2 * GLA_HEADS * GLA_DK + 2 * gw) // pwid
    return pl.pallas_call(
        functools.partial(_mix0_out_kernel, n_lat=n_lat, n_ctx=n_ctx, tm=tm),
        grid=(rows // tm,),
        in_specs=[pl.BlockSpec((tm, d), lambda i: (i, 0)),
                  pl.BlockSpec((2, 3, d), lambda i: (0, 0, 0)),
                  pl.BlockSpec((tm, gw), lambda i: (i, 0)),
                  pl.BlockSpec((tm, gw), lambda i: (i, 0)),
                  pl.BlockSpec((tm, gw), lambda i: (i, r_col)),
                  pl.BlockSpec((POOL_HALO, pwid), lambda i: (jnp.maximum(i * hb - 1, 0), u_col)),
                  pl.BlockSpec((tm, pwid), lambda i: (i, u_col)),
                  pl.BlockSpec((POOL_HALO, pwid),
                               lambda i: (jnp.minimum((i + 1) * hb, n_hblk - 1), u_col)),
                  pl.BlockSpec((1, GLA_DV), lambda i: (0, 0)),
                  pl.BlockSpec((len(POOL_WINDOWS), POOL_GC, POOL_GC), lambda i: (0, 0, 0)),
                  pl.BlockSpec((1, pwid), lambda i: (0, 0)),
                  pl.BlockSpec((d, d), lambda i: (0, 0))],
        out_specs=pl.BlockSpec((tm, d), lambda i: (i, 0)),
        out_shape=jax.ShapeDtypeStruct((rows, d), F32),
        scratch_shapes=[pltpu.VMEM((tm + 2 * POOL_HALO, pwid), F32)],
        compiler_params=_cparams(("arbitrary",), 48),
        name="mix0_readout",
    )(h, mod3, o_f, o_b, p0, p0, p0, p0, norm_w.reshape(1, GLA_DV), pool_w,
      pool_scale.reshape(1, pwid), w_out)


def _proj1_kernel(h_ref, mod_ref, w_ref, cos_ref, sin_ref, qk_ref, vt_ref, z_ref,
                  *, n_lat, tm, tn, n_qt, qscale):
    i = pl.program_id(0)
    j = pl.program_id(1)

    @pl.when(j == 0)
    def _():
        _store_modulated(h_ref, mod_ref, z_ref, i * tm, tm, n_lat)

    @pl.when(j < 2 * n_qt)
    def _():
        acc = jnp.dot(z_ref[...], w_ref[...], preferred_element_type=F32)
        scale = jnp.where(j < n_qt, qscale, 1.0)
        cs = cos_ref[...] * scale
        sn = sin_ref[...] * scale
        for gidx in range(tn // LANES):
            x = acc[:, gidx * LANES:(gidx + 1) * LANES]
            y = x * cs + pltpu.roll(x, LANES // 2, 1) * sn
            qk_ref[:, gidx * LANES:(gidx + 1) * LANES] = y.astype(BF16)

    @pl.when(j >= 2 * n_qt)
    def _():
        vt = jnp.dot(z_ref[...], w_ref[...], preferred_element_type=F32).T
        for hd in range(tn // DIFF_DV):
            vt_ref[hd] = vt[hd * DIFF_DV:(hd + 1) * DIFF_DV].astype(BF16)


def _proj1(h, mod3, w_qkv, cos_t, sin_t, *, tm, n_lat):
    rows, d = h.shape
    tn = 1024
    n_qt = d // tn
    n_vt = w_qkv.shape[1] // tn - 2 * n_qt
    hpt = tn // DIFF_DV
    qscale = (DIFF_DH ** -0.5) * math.log2(math.e)
    return pl.pallas_call(
        functools.partial(_proj1_kernel, n_lat=n_lat, tm=tm, tn=tn, n_qt=n_qt, qscale=qscale),
        grid=(rows // tm, 2 * n_qt + n_vt),
        in_specs=[pl.BlockSpec((tm, d), lambda i, j: (i, 0)),
                  pl.BlockSpec((2, 3, d), lambda i, j: (0, 0, 0)),
                  pl.BlockSpec((d, tn), lambda i, j: (0, j)),
                  pl.BlockSpec((tm, LANES), lambda i, j: (i, 0)),
                  pl.BlockSpec((tm, LANES), lambda i, j: (i, 0))],
        out_specs=[pl.BlockSpec((tm, tn), lambda i, j: (i, jnp.minimum(j, 2 * n_qt - 1))),
                   pl.BlockSpec((hpt, None, DIFF_DV, tm),
                                lambda i, j: (jnp.maximum(j - 2 * n_qt, 0), i, 0, 0))],
        out_shape=[jax.ShapeDtypeStruct((rows, 2 * d), BF16),
                   jax.ShapeDtypeStruct((DIFF_HEADS, rows // tm, DIFF_DV, tm), BF16)],
        scratch_shapes=[pltpu.VMEM((tm, d), BF16)],
        compiler_params=_cparams(("arbitrary", "arbitrary"), 52),
        name="proj_qkv_rope",
    )(h, mod3, w_qkv, cos_t, sin_t)


ATT_TILE = 256


def _attn_kernel(lam_ref, q_ref, k_ref, vt_ref, nw_ref, o_ref, m_ref, l_ref, acc_ref,
                 sa_ref, sb_ref, ca_ref, cb_ref, p_ref, *, tk, n_kv, lam_init):
    tq = q_ref.shape[0]
    T = ATT_TILE
    SUB = 8
    m_ref[...] = jnp.full_like(m_ref, -jnp.inf)
    l_ref[...] = jnp.zeros_like(l_ref)
    acc_ref[...] = jnp.zeros_like(acc_ref)

    units = [(c, qb) for qb in range(tq // T) for c in range(2)]

    def scores(jj, s_ref, c_ref, c, qb):
        comp = slice(c * DIFF_DH, (c + 1) * DIFF_DH)
        cols = slice(qb * T, (qb + 1) * T)
        qc = q_ref[cols, comp]
        cmax = None
        for kb in range(tk // T):
            kc = k_ref[jj * tk + kb * T:jj * tk + (kb + 1) * T, comp]
            st = lax.dot_general(kc, qc, (((1,), (1,)), ((), ())), preferred_element_type=F32)
            s_ref[c, kb * T:(kb + 1) * T, cols] = st
            bmax = jnp.max(st.reshape(T // SUB, SUB, T), axis=0)
            cmax = bmax if cmax is None else jnp.maximum(cmax, bmax)
        c_ref[c, :, cols] = cmax

    def accumulate(jj, s_ref, c_ref, c, qb):
        cols = slice(qb * T, (qb + 1) * T)
        m_old = m_ref[c, :, cols]
        m_new = jnp.maximum(m_old, jnp.max(c_ref[c, :, cols], axis=0, keepdims=True))
        alpha = jnp.exp2(m_old - m_new)
        lsum = jnp.zeros((SUB, T), F32)
        for kb in range(tk // T):
            keys = slice(kb * T, (kb + 1) * T)
            pt = jnp.exp2(s_ref[c, keys, cols] - m_new)
            lsum = lsum + jnp.sum(pt.reshape(T // SUB, SUB, T), axis=0)
            p_ref[c, keys, cols] = pt.astype(BF16)
        pv = jnp.dot(vt_ref[jj], p_ref[c, :, cols], preferred_element_type=F32)
        l_ref[c, :, cols] = alpha * l_ref[c, :, cols] + lsum
        acc_ref[c, :, cols] = alpha * acc_ref[c, :, cols] + pv
        m_ref[c, :, cols] = m_new

    slots = ((sa_ref, ca_ref), (sb_ref, cb_ref))
    for c, qb in units:
        scores(0, *slots[0], c, qb)
    for jj in range(n_kv):
        for c, qb in units:
            accumulate(jj, *slots[jj % 2], c, qb)
            if jj + 1 < n_kv:
                scores(jj + 1, *slots[(jj + 1) % 2], c, qb)

    lp = lam_ref[...]
    lam = (jnp.exp(jnp.sum(lp[0:1] * lp[1:2], axis=-1, keepdims=True))
           - jnp.exp(jnp.sum(lp[2:3] * lp[3:4], axis=-1, keepdims=True)) + lam_init)
    l0 = jnp.sum(l_ref[0], axis=0, keepdims=True)
    l1 = jnp.sum(l_ref[1], axis=0, keepdims=True)
    ot = acc_ref[0] / l0 - lam * (acc_ref[1] / l1)
    ms = jnp.mean(ot * ot, axis=0, keepdims=True)
    ont = ot * lax.rsqrt(ms + NORM_EPS) * (nw_ref[...] * (1.0 - lam_init))
    o_ref[...] = ont.T.astype(BF16)


def _diff_attn(qk, vt, lam_p, norm_w, *, n_lat, lam_init):
    rows = qk.shape[0]
    d = DIFF_HEADS * DIFF_DV
    n_kv, tk = vt.shape[1], vt.shape[3]
    tq = 512
    return pl.pallas_call(
        functools.partial(_attn_kernel, tk=tk, n_kv=n_kv, lam_init=lam_init),
        grid=(DIFF_HEADS, n_lat // tq),
        in_specs=[pl.BlockSpec((4, DIFF_DH), lambda h, i: (0, 0)),
                  pl.BlockSpec((tq, DIFF_DV), lambda h, i: (i, h)),
                  pl.BlockSpec((rows, DIFF_DV), lambda h, i: (0, DIFF_HEADS + h)),
                  pl.BlockSpec((None, n_kv, DIFF_DV, tk), lambda h, i: (h, 0, 0, 0)),
                  pl.BlockSpec((DIFF_DV, 1), lambda h, i: (0, 0))],
        out_specs=pl.BlockSpec((tq, DIFF_DV), lambda h, i: (i, h)),
        out_shape=jax.ShapeDtypeStruct((n_lat, d), BF16),
        scratch_shapes=[pltpu.VMEM((2, 1, tq), F32), pltpu.VMEM((2, 8, tq), F32),
                        pltpu.VMEM((2, DIFF_DV, tq), F32),
                        pltpu.VMEM((2, tk, tq), F32), pltpu.VMEM((2, tk, tq), F32),
                        pltpu.VMEM((2, 8, tq), F32), pltpu.VMEM((2, 8, tq), F32),
                        pltpu.VMEM((2, tk, tq), BF16)],
        compiler_params=_cparams(("arbitrary", "arbitrary"), 48),
        name="diff_attn",
    )(lam_p, qk, qk, vt, norm_w.reshape(DIFF_DV, 1))


def _outproj_kernel(h_ref, mod_ref, x_ref, w_ref, o_ref):
    y = jnp.dot(x_ref[...], w_ref[...], preferred_element_type=F32)
    o_ref[...] = h_ref[...] + mod_ref[0, 2:3, :] * y


def _outproj_latent(h, mod3, x, w, *, n_lat):
    d = h.shape[1]
    tm = 512
    return pl.pallas_call(
        _outproj_kernel,
        grid=(n_lat // tm,),
        in_specs=[pl.BlockSpec((tm, d), lambda i: (i, 0)),
                  pl.BlockSpec((2, 3, d), lambda i: (0, 0, 0)),
                  pl.BlockSpec((tm, d), lambda i: (i, 0)),
                  pl.BlockSpec((d, d), lambda i: (0, 0))],
        out_specs=pl.BlockSpec((tm, d), lambda i: (i, 0)),
        out_shape=jax.ShapeDtypeStruct((n_lat, d), F32),
        compiler_params=_cparams(("arbitrary",), 40),
        name="attn_outproj",
    )(h, mod3, x, w)


def _rope_tables(n_lat, n_ctx):
    rows = n_lat // GRID_W
    row = jnp.repeat(jnp.arange(rows), GRID_W).astype(F32)
    col = jnp.tile(jnp.arange(GRID_W), rows).astype(F32)
    n_freq = DIFF_DH // 4
    inv_freq = ROPE_THETA ** (-jnp.arange(n_freq, dtype=F32) / n_freq)
    ang = jnp.concatenate([row[:, None] * inv_freq, col[:, None] * inv_freq], axis=-1)
    cos, sin = jnp.cos(ang), jnp.sin(ang)
    cos_t = jnp.concatenate([cos, cos], axis=-1)
    sin_t = jnp.concatenate([-sin, sin], axis=-1)
    cos_t = jnp.concatenate([cos_t, jnp.ones((n_ctx, DIFF_DH), F32)], axis=0)
    sin_t = jnp.concatenate([sin_t, jnp.zeros((n_ctx, DIFF_DH), F32)], axis=0)
    return cos_t, sin_t


def kernel(x, c, ctx, c_ctx, ada_w, ada_b, ffn_w1, ffn_w3, ffn_w2, gla_w_in, gla_gate_w2,
           gla_gate_b, gla_norm_w, pool_w, pool_scale, mix0_w_out, diff_w_qkv, diff_lambda,
           diff_norm_w, diff_w_out, final_norm_w):
    assert x.shape[0] == 1 and ada_w.shape[0] == 2
    n_lat, d = x.shape[1], x.shape[2]
    n_ctx = ctx.shape[1]
    rows = n_lat + n_ctx
    tm = 768
    assert rows % tm == 0 and n_lat % 512 == 0 and n_ctx % 256 == 0 and n_lat % GRID_W == 0

    h = jnp.concatenate([x[0], ctx[0]], axis=0)
    c8 = jnp.zeros((8, d), F32).at[0].set(c[0]).at[1].set(c_ctx)
    mods = _ada(c8, ada_w, ada_b)[:, :2].reshape(2, 2, 9, d)

    def first_weights(layer, half):
        return tuple(wt[layer, half].astype(BF16) for wt in (ffn_w1, ffn_w3, ffn_w2))

    def following(layer, half):
        return (ffn_w1, ffn_w3, ffn_w2, layer, half)

    m = mods[0]
    h, w_next = _ffn(h, m[:, 0:3], first_weights(0, 0), rows=rows, tm=tm, n_lat=n_lat,
                     nxt=following(0, 1))
    w_in = gla_w_in[0]
    n_qkvr = 2 * GLA_HEADS * GLA_DK + 2 * GLA_HEADS * GLA_DV
    n_gate = 2 * GLA_GATE_RANK
    w_a = w_in[:, :n_qkvr].astype(BF16)
    w_u = w_in[:, n_qkvr + n_gate:].astype(BF16)
    w_gate = jnp.pad(w_in[:, n_qkvr:n_qkvr + n_gate], ((0, 0), (0, LANES - n_gate))).astype(BF16)
    gw2 = gla_gate_w2[0].astype(BF16)
    gw2p = jnp.zeros((2, LANES, gw2.shape[-1]), BF16)
    gw2p = gw2p.at[0, :GLA_GATE_RANK].set(gw2[0]).at[1, GLA_GATE_RANK:n_gate].set(gw2[1])
    p0, v16, lf, lb = _proj0(h, m[:, 3:6], w_a, w_u, w_gate, gw2p, gla_gate_b[0], tm=tm,
                             n_lat=n_lat)
    o_f, o_b = _gla(p0, v16, lf, lb, n_lat=n_lat)
    h = _mix0_out(h, m[:, 3:6], o_f, o_b, p0, gla_norm_w[0], pool_w[0].astype(BF16),
                  pool_scale[0], mix0_w_out[0].astype(BF16), n_lat=n_lat, n_ctx=n_ctx)
    h, w_next = _ffn(h, m[:, 6:9], w_next, rows=rows, tm=tm, n_lat=n_lat, nxt=following(1, 0))

    m = mods[1]
    h, w_next = _ffn(h, m[:, 0:3], w_next, rows=rows, tm=tm, n_lat=n_lat, nxt=following(1, 1))
    cos_t, sin_t = _rope_tables(n_lat, n_ctx)
    qk, vt = _proj1(h, m[:, 3:6], diff_w_qkv[0].astype(BF16), cos_t, sin_t, tm=tm, n_lat=n_lat)
    lam_init = 0.8 - 0.6 * math.exp(-0.3 * 1)
    o = _diff_attn(qk, vt, diff_lambda[0], diff_norm_w[0], n_lat=n_lat, lam_init=lam_init)
    hl = _outproj_latent(h, m[:, 3:6], o, diff_w_out[0].astype(BF16), n_lat=n_lat)
    out = _ffn(hl, m[:, 6:9], w_next, rows=n_lat, tm=512, n_lat=n_lat, final_w=final_norm_w)
    return out[None]
```
